```python
import math
import functools
import jax
import jax.numpy as jnp
from jax import lax
import numpy as np

D_MODEL = 2048
BATCH = 4
SEQ = 4096
DEPTH = 2
DEC_BATCH = 128
DEC_SEQ = 8
PAST_LEN = 16384
PAGE_SIZE = 128

SSM_WIDTH = D_MODEL // 4
SSM_GROUP = 16
SSM_GROUPS = SSM_WIDTH // SSM_GROUP
SSM_STATE = 64
MLA_HEADS = D_MODEL // 256
MLA_NOPE = 128
MLA_ROPE = 64
MLA_QK = MLA_NOPE + MLA_ROPE
MLA_V = 128
MLA_Q_LORA = D_MODEL // 4
MLA_KV_LORA = D_MODEL // 8
MLA_WIDTH = MLA_HEADS * MLA_V
MOBA_HEADS = D_MODEL // 512
MOBA_DH = 128
MOBA_WIDTH = MOBA_HEADS * MOBA_DH
MOBA_BLOCK = 256
MOBA_TOPK = 3
D_MIX = SSM_WIDTH + MLA_WIDTH + MOBA_WIDTH
IN_COLS = SSM_WIDTH + MLA_Q_LORA + MLA_KV_LORA + MLA_ROPE + MOBA_WIDTH + 2 * MOBA_DH
D_FF = ((8 * D_MODEL // 3 + 127) // 128) * 128
N_EXPERTS = 8
TOP_K = 2
D_FF_EXPERT = 7 * D_MODEL // 2
MOE_BLOCK = 256
Q_BLOCK = 128
ROW_CHUNK = 64
ROPE_THETA = 10000.0
NORM_EPS = 1e-6
N_DENSE = (DEPTH + 1) // 2
N_MOE = DEPTH // 2
F32 = jnp.float32

kernel_name = 'hybrid_s5_mla_moba_decode_step'


def _rms(x, g):
    xf = x.astype(F32)
    y = xf * lax.rsqrt(jnp.mean(xf * xf, axis=-1, keepdims=True) + NORM_EPS)
    return (y * g.astype(F32)).astype(x.dtype)


def _rope_cs(pos):
    inv = ROPE_THETA ** (-jnp.arange(0, MLA_ROPE, 2, dtype=F32) / MLA_ROPE)
    ang = pos.astype(F32)[:, None] * inv[None, :]
    return jnp.cos(ang), jnp.sin(ang)


def _rope(x, cos, sin):
    half = MLA_ROPE // 2
    xf = x.astype(F32)
    x1, x2 = xf[..., :half], xf[..., half:]
    return jnp.concatenate([x1 * cos - x2 * sin, x1 * sin + x2 * cos], axis=-1).astype(x.dtype)


def _alibi_slopes():
    return jnp.asarray(2.0 ** (-8.0 * np.arange(1, MOBA_HEADS + 1) / MOBA_HEADS), dtype=F32)


def _split_points():
    sizes = [SSM_WIDTH, MLA_Q_LORA, MLA_KV_LORA, MLA_ROPE, MOBA_WIDTH, MOBA_DH, MOBA_DH]
    return np.cumsum(sizes)[:-1].tolist()


def _lin_combine(e1, e2):
    a1, b1 = e1
    a2, b2 = e2
    return a1 * a2, a2 * b1 + b2


def _s5(u, h0, lam_re, lam_im, log_dt, b_re, b_im, c_re, c_im, d_skip, w_glu, b_glu):
    n, L, _ = u.shape
    u32 = u.astype(F32).reshape(n, L, SSM_GROUPS, SSM_GROUP)
    lam = lax.complex(lam_re.astype(F32), lam_im.astype(F32))
    dt = jnp.exp(log_dt.astype(F32))[:, None]
    lam_bar = jnp.exp(lam * dt)
    b_bar = ((lam_bar - 1.0) / lam)[:, :, None] * lax.complex(b_re.astype(F32), b_im.astype(F32))
    bu = jnp.einsum('gpc,nlgc->nlgp', b_bar, u32.astype(jnp.complex64))
    a = jnp.broadcast_to(lam_bar, bu.shape)
    a_cum, hs = lax.associative_scan(_lin_combine, (a, bu), axis=1)
    hs = hs + a_cum * h0[:, None]
    c = lax.complex(c_re.astype(F32), c_im.astype(F32))
    y = jnp.real(jnp.einsum('gcp,nlgp->nlgc', c, hs)) + d_skip.astype(F32).reshape(SSM_GROUPS, SSM_GROUP) * u32
    z = jax.nn.gelu(y.reshape(n, L, SSM_WIDTH))
    out = z * jax.nn.sigmoid(z @ w_glu.astype(F32) + b_glu.astype(F32))
    return out.astype(u.dtype), hs[:, -1]


def _mla_gain(g_nope, g_rope):
    return jnp.concatenate([g_nope, g_rope, g_rope], axis=-1)


def _mla_q(cq_raw, q_a_norm, w_uq, g_nope, g_rope, cos, sin):
    cq = _rms(cq_raw, q_a_norm)
    q = (cq @ w_uq).reshape(cq.shape[:-1] + (MLA_HEADS, MLA_QK))
    q = jnp.concatenate([q[..., :MLA_NOPE], _rope(q[..., MLA_NOPE:], cos[:, None], sin[:, None])], axis=-1)
    return _rms(q, _mla_gain(g_nope, g_rope))


def _mla_k(ckv, kr, w_uk, g_nope, g_rope):
    k_nope = jnp.einsum('...c,chd->...hd', ckv, w_uk)
    kr_h = jnp.broadcast_to(kr[..., None, :], k_nope.shape[:-1] + (MLA_ROPE,))
    return _rms(jnp.concatenate([k_nope, kr_h], axis=-1), _mla_gain(g_nope, g_rope))


def _mla_prompt(q, k, v):
    B, S = q.shape[:2]
    nqb = S // Q_BLOCK
    scale = MLA_QK ** -0.5
    qb = q.reshape(B, nqb, Q_BLOCK, MLA_HEADS, MLA_QK).transpose(1, 0, 2, 3, 4)
    kpos = jnp.arange(S)

    def block(args):
        qi, i = args
        s = jnp.einsum('bqhd,bkhd->bhqk', qi, k, preferred_element_type=F32) * scale
        qpos = i * Q_BLOCK + jnp.arange(Q_BLOCK)
        s = jnp.where(kpos[None, :] <= qpos[:, None], s, -jnp.inf)
        p = jax.nn.softmax(s, axis=-1)
        return jnp.einsum('bhqk,bkhd->bqhd', p.astype(v.dtype), v)

    out = lax.map(block, (qb, jnp.arange(nqb)))
    return out.transpose(1, 0, 2, 3, 4).reshape(B, S, MLA_WIDTH)


def _mla_sample(l, q, ckv_new, kr_new, cache_ckv, cache_kr, page_table, w_uk, w_uv, g_nope, g_rope):
    n, L = q.shape[:2]
    scale = MLA_QK ** -0.5

    def step(carry, phys):
        m, den, acc = carry
        c = cache_ckv[l, phys]
        k = _mla_k(c, cache_kr[l, phys], w_uk, g_nope, g_rope)
        s = jnp.einsum('bqhd,bshd->bhqs', q, k, preferred_element_type=F32) * scale
        m_new = jnp.maximum(m, s.max(-1))
        alpha = jnp.exp(m - m_new)
        p = jnp.exp(s - m_new[..., None])
        acc = acc * alpha[..., None] + jnp.einsum('bhqs,bsc->bhqc', p, c.astype(F32))
        return (m_new, den * alpha + p.sum(-1), acc), None

    init = (jnp.full((n, MLA_HEADS, L), -jnp.inf, F32),
            jnp.zeros((n, MLA_HEADS, L), F32),
            jnp.zeros((n, MLA_HEADS, L, MLA_KV_LORA), F32))
    (m, den, acc), _ = lax.scan(step, init, page_table.T)
    k = _mla_k(ckv_new, kr_new, w_uk, g_nope, g_rope)
    s = jnp.einsum('bqhd,bshd->bhqs', q, k, preferred_element_type=F32) * scale
    s = jnp.where(jnp.tril(jnp.ones((L, L), bool)), s, -jnp.inf)
    m_new = jnp.maximum(m, s.max(-1))
    alpha = jnp.exp(m - m_new)
    p = jnp.exp(s - m_new[..., None])
    den = den * alpha + p.sum(-1)
    acc = acc * alpha[..., None] + jnp.einsum('bhqs,bsc->bhqc', p, ckv_new.astype(F32))
    out = jnp.einsum('bhqc,chd->bqhd', acc / den[..., None], w_uv.astype(F32))
    return out.reshape(n, L, MLA_WIDTH).astype(q.dtype)


def _map_rows(fn, xs, n_rows):
    pad = (-n_rows) % ROW_CHUNK

    def prep(a):
        a = jnp.pad(a, [(0, pad)] + [(0, 0)] * (a.ndim - 1))
        return a.reshape((-1, ROW_CHUNK) + a.shape[1:])

    out = lax.map(fn, tuple(prep(a) for a in xs))
    return out.reshape((-1,) + out.shape[2:])[:n_rows]


def _moba_select(q, kmean, jb):
    nb = kmean.shape[1]
    g = jnp.einsum('nlhd,nbd->nlhb', q.astype(F32), kmean)
    g = jnp.where(jnp.arange(nb)[None, None, None, :] < jb[None, :, None, None], g, -jnp.inf)
    if nb < MOBA_TOPK:
        g = jnp.pad(g, [(0, 0)] * 3 + [(0, MOBA_TOPK - nb)], constant_values=-jnp.inf)
    _, sel = lax.top_k(g, MOBA_TOPK)
    return jnp.clip(sel, 0, max(nb - 1, 0)).astype(jnp.int32)


def _moba_core(q, ksel, vsel, pos_sel, ok_sel, kown, vown, pos_own, ok_own, t):
    slopes = _alibi_slopes()[None, :, None]
    scale = MOBA_DH ** -0.5
    tf = t.astype(F32)[:, None, None]
    s1 = jnp.einsum('rhd,rhkd->rhk', q, ksel, preferred_element_type=F32) * scale - slopes * (tf - pos_sel.astype(F32))
    s1 = jnp.where(ok_sel, s1, -jnp.inf)
    s2 = jnp.einsum('rhd,rkd->rhk', q, kown, preferred_element_type=F32) * scale - slopes * (tf - pos_own.astype(F32)[:, None, :])
    s2 = jnp.where(ok_own[:, None, :], s2, -jnp.inf)
    p = jax.nn.softmax(jnp.concatenate([s1, s2], axis=-1), axis=-1)
    n1 = s1.shape[-1]
    out = jnp.einsum('rhk,rhkd->rhd', p[..., :n1], vsel.astype(F32)) + jnp.einsum('rhk,rkd->rhd', p[..., n1:], vown.astype(F32))
    return out.astype(q.dtype)


def _sel_masks(sr, jbr):
    R = sr.shape[0]
    off = jnp.arange(MOBA_BLOCK)
    pos_sel = (sr[..., None] * MOBA_BLOCK + off).reshape(R, MOBA_HEADS, MOBA_TOPK * MOBA_BLOCK)
    ok = jnp.arange(MOBA_TOPK)[None, None, :] < jbr[:, None, None]
    ok_sel = jnp.broadcast_to(ok[..., None], sr.shape + (MOBA_BLOCK,)).reshape(R, MOBA_HEADS, MOBA_TOPK * MOBA_BLOCK)
    return pos_sel, ok_sel


def _moba_prompt(q, k, v):
    B, S = q.shape[:2]
    nb = -(-S // MOBA_BLOCK)
    pad = nb * MOBA_BLOCK - S
    kblk = jnp.pad(k, ((0, 0), (0, pad), (0, 0))).reshape(B, nb, MOBA_BLOCK, MOBA_DH)
    vblk = jnp.pad(v, ((0, 0), (0, pad), (0, 0))).reshape(B, nb, MOBA_BLOCK, MOBA_DH)
    kmean = kblk.astype(F32).mean(axis=2)
    t_all = jnp.arange(S)
    sel = _moba_select(q, kmean, t_all // MOBA_BLOCK)
    off = jnp.arange(MOBA_BLOCK)

    def chunk(args):
        qr, br, tr, sr = args
        R = qr.shape[0]
        jbr = tr // MOBA_BLOCK
        ksel = kblk[br[:, None, None], sr].reshape(R, MOBA_HEADS, MOBA_TOPK * MOBA_BLOCK, MOBA_DH)
        vsel = vblk[br[:, None, None], sr].reshape(R, MOBA_HEADS, MOBA_TOPK * MOBA_BLOCK, MOBA_DH)
        pos_sel, ok_sel = _sel_masks(sr, jbr)
        pos_own = jbr[:, None] * MOBA_BLOCK + off
        return _moba_core(qr, ksel, vsel, pos_sel, ok_sel, kblk[br, jbr], vblk[br, jbr],
                          pos_own, pos_own <= tr[:, None], tr)

    rows = B * S
    out = _map_rows(chunk, (q.reshape(rows, MOBA_HEADS, MOBA_DH), jnp.repeat(jnp.arange(B), S),
                            jnp.tile(t_all, B), sel.reshape(rows, MOBA_HEADS, MOBA_TOPK)), rows)
    return out.reshape(B, S, MOBA_WIDTH)


def _moba_sample(l, q, k_new, v_new, cache_k, cache_v, page_table):
    n, L = q.shape[:2]
    n_pages = PAST_LEN // PAGE_SIZE
    ppb = MOBA_BLOCK // PAGE_SIZE
    nbp = PAST_LEN // MOBA_BLOCK
    if nbp > 0:
        def bmean(j):
            phys = lax.dynamic_slice_in_dim(page_table, j * ppb, ppb, axis=1)
            return cache_k[l, phys].astype(F32).mean(axis=(1, 2))
        kmean = lax.map(bmean, jnp.arange(nbp)).transpose(1, 0, 2)
    else:
        kmean = jnp.zeros((n, 0, MOBA_DH), F32)
    sel = _moba_select(q, kmean, (PAST_LEN + jnp.arange(L)) // MOBA_BLOCK)
    off = jnp.arange(MOBA_BLOCK)
    pos_n = PAST_LEN + jnp.arange(L)

    def chunk(args):
        qr, br, ir, sr = args
        R = qr.shape[0]
        tr = PAST_LEN + ir
        jbr = tr // MOBA_BLOCK
        lp = jnp.clip(sr[..., None] * ppb + jnp.arange(ppb), 0, n_pages - 1)
        phys = page_table[br[:, None, None, None], lp]
        ksel = cache_k[l, phys].reshape(R, MOBA_HEADS, MOBA_TOPK * MOBA_BLOCK, MOBA_DH)
        vsel = cache_v[l, phys].reshape(R, MOBA_HEADS, MOBA_TOPK * MOBA_BLOCK, MOBA_DH)
        pos_sel, ok_sel = _sel_masks(sr, jbr)
        lpo = jnp.clip(jbr[:, None] * ppb + jnp.arange(ppb), 0, n_pages - 1)
        physo = page_table[br[:, None], lpo]
        kown_c = cache_k[l, physo].reshape(R, MOBA_BLOCK, MOBA_DH)
        vown_c = cache_v[l, physo].reshape(R, MOBA_BLOCK, MOBA_DH)
        pos_c = jbr[:, None] * MOBA_BLOCK + off
        ok_c = pos_c < PAST_LEN
        pos_nr = jnp.broadcast_to(pos_n[None, :], (R, L))
        ok_n = (pos_nr <= tr[:, None]) & (pos_nr >= (jbr * MOBA_BLOCK)[:, None])
        kown = jnp.concatenate([kown_c, k_new[br]], axis=1)
        vown = jnp.concatenate([vown_c, v_new[br]], axis=1)
        return _moba_core(qr, ksel, vsel, pos_sel, ok_sel, kown, vown,
                          jnp.concatenate([pos_c, pos_nr], axis=1), jnp.concatenate([ok_c, ok_n], axis=1), tr)

    rows = n * L
    out = _map_rows(chunk, (q.reshape(rows, MOBA_HEADS, MOBA_DH), jnp.repeat(jnp.arange(n), L),
                            jnp.tile(jnp.arange(L), n), sel.reshape(rows, MOBA_HEADS, MOBA_TOPK)), rows)
    return out.reshape(n, L, MOBA_WIDTH)


def _mixer(x, lw, cos, sin, h0, mla_attend, moba_attend):
    n, L, _ = x.shape
    h = _rms(x, lw['norm_mix'])
    proj = jnp.einsum('nld,dc->nlc', h, lw['w_in'])
    u, cq, ckv_raw, kr_raw, qm, km, vm = jnp.split(proj, _split_points(), axis=-1)
    ssm_out, ssm_last = _s5(u, h0, lw['lam_re'], lw['lam_im'], lw['log_dt'], lw['b_re'], lw['b_im'],
                            lw['c_re'], lw['c_im'], lw['d'], lw['w_glu'], lw['b_glu'])
    q = _mla_q(cq, lw['q_a_norm'], lw['w_uq'], lw['qn_nope'], lw['qn_rope'], cos, sin)
    ckv = _rms(ckv_raw, lw['kv_a_norm'])
    kr = _rope(kr_raw, cos, sin)
    mla_out = mla_attend(q, ckv, kr)
    qm = _rms(qm.reshape(n, L, MOBA_HEADS, MOBA_DH), lw['moba_qn'])
    km = _rms(km, lw['moba_kn'])
    moba_out = moba_attend(qm, km, vm)
    o = jnp.concatenate([_rms(ssm_out, lw['g_ssm']), _rms(mla_out, lw['g_mla']), _rms(moba_out, lw['g_moba'])], axis=-1)
    x = x + jnp.einsum('nlc,cd->nld', o, lw['w_out'])
    return x, (ckv, kr, km, vm, ssm_last)


def _mla_prompt_attend(q, ckv, kr, w_uk, w_uv, g_nope, g_rope):
    k = _mla_k(ckv, kr, w_uk, g_nope, g_rope)
    v = jnp.einsum('nlc,chd->nlhd', ckv, w_uv)
    return _mla_prompt(q, k, v)


def _swiglu(h, w_g, w_u, w_d):
    return (jax.nn.silu(h @ w_g) * (h @ w_u)) @ w_d


def _moe(h, w_r, b_r, w_g, w_u, w_d):
    T = h.shape[0]
    logits = jnp.dot(h, w_r, preferred_element_type=F32) + b_r.astype(F32)
    top_logit, top_e = lax.top_k(logits, TOP_K)
    gates = jax.nn.softmax(top_logit, axis=-1)
    n_assign = T * TOP_K
    flat_e = top_e.reshape(n_assign)
    order = jnp.argsort(flat_e)
    sorted_e = flat_e[order]
    counts = jnp.bincount(flat_e, length=N_EXPERTS)
    padded = (counts + MOE_BLOCK - 1) // MOE_BLOCK * MOE_BLOCK
    ends = jnp.cumsum(padded)
    dest_sorted = ((ends - padded)[sorted_e] + jnp.arange(n_assign) - (jnp.cumsum(counts) - counts)[sorted_e]).astype(jnp.int32)
    n_rows = -(-n_assign // MOE_BLOCK) * MOE_BLOCK + N_EXPERTS * MOE_BLOCK
    n_blocks = n_rows // MOE_BLOCK
    row_tok = jnp.full((n_rows,), T, jnp.int32).at[dest_sorted].set((order // TOP_K).astype(jnp.int32))
    block_e = jnp.minimum(jnp.searchsorted(ends, jnp.arange(n_blocks) * MOE_BLOCK, side='right'), N_EXPERTS - 1)
    xs = jnp.concatenate([h, jnp.zeros((1, h.shape[1]), h.dtype)], axis=0)[row_tok].reshape(n_blocks, MOE_BLOCK, -1)

    def expert_block(args):
        xb, e = args
        return (jax.nn.silu(xb @ w_g[e]) * (xb @ w_u[e])) @ w_d[e]

    ys = lax.map(expert_block, (xs, block_e)).reshape(n_rows, -1)
    dest = jnp.zeros((n_assign,), jnp.int32).at[order].set(dest_sorted)
    return jnp.einsum('tk,tkd->td', gates.astype(h.dtype), ys[dest].reshape(T, TOP_K, -1))


def setup_inputs(seed: int = 0) -> dict:
    key = jax.random.key(seed)
    ks = iter(jax.random.split(key, 64))

    def nrm(shape, scale):
        return jax.random.normal(next(ks), shape, F32) * scale

    def gain(shape):
        return 1.0 + nrm(shape, 0.05)

    n_pages = PAST_LEN // PAGE_SIZE
    n_pool = (DEC_BATCH * n_pages * 5) // 4
    perm = jax.random.permutation(next(ks), n_pool).astype(jnp.int32)
    page_table = perm[:DEC_BATCH * n_pages].reshape(DEC_BATCH, n_pages)
    G, P, C = SSM_GROUPS, SSM_STATE, SSM_GROUP
    return {
        'x_prompt': nrm((BATCH, SEQ, D_MODEL), 1.0),
        'x_sample': nrm((DEC_BATCH, DEC_SEQ, D_MODEL), 1.0),
        'cache_mla_ckv': nrm((DEPTH, n_pool, PAGE_SIZE, MLA_KV_LORA), 1.0),
        'cache_mla_krope': nrm((DEPTH, n_pool, PAGE_SIZE, MLA_ROPE), 1.0),
        'cache_moba_k': nrm((DEPTH, n_pool, PAGE_SIZE, MOBA_DH), 1.0),
        'cache_moba_v': nrm((DEPTH, n_pool, PAGE_SIZE, MOBA_DH), 1.0),
        'state_ssm_re': nrm((DEPTH, DEC_BATCH, G, P), 0.5),
        'state_ssm_im': nrm((DEPTH, DEC_BATCH, G, P), 0.5),
        'page_table': page_table,
        'norm_mix': gain((DEPTH, D_MODEL)),
        'w_in': nrm((DEPTH, D_MODEL, IN_COLS), D_MODEL ** -0.5),
        'ssm_lam_re': -0.5 + nrm((DEPTH, G, P), 0.01),
        'ssm_lam_im': math.pi * jnp.arange(P, dtype=F32) + nrm((DEPTH, G, P), 0.01),
        'ssm_log_dt': jax.random.uniform(next(ks), (DEPTH, G), F32, math.log(1e-3), math.log(1e-1)),
        'ssm_b_re': nrm((DEPTH, G, P, C), (2 * C) ** -0.5),
        'ssm_b_im': nrm((DEPTH, G, P, C), (2 * C) ** -0.5),
        'ssm_c_re': nrm((DEPTH, G, C, P), P ** -0.5),
        'ssm_c_im': nrm((DEPTH, G, C, P), P ** -0.5),
        'ssm_d': nrm((DEPTH, SSM_WIDTH), 1.0),
        'ssm_w_glu': nrm((DEPTH, SSM_WIDTH, SSM_WIDTH), SSM_WIDTH ** -0.5),
        'ssm_b_glu': nrm((DEPTH, SSM_WIDTH), 0.02),
        'mla_q_a_norm': gain((DEPTH, MLA_Q_LORA)),
        'mla_kv_a_norm': gain((DEPTH, MLA_KV_LORA)),
        'mla_w_uq': nrm((DEPTH, MLA_Q_LORA, MLA_HEADS * MLA_QK), MLA_Q_LORA ** -0.5),
        'mla_w_uk': nrm((DEPTH, MLA_KV_LORA, MLA_HEADS, MLA_NOPE), MLA_KV_LORA ** -0.5),
        'mla_w_uv': nrm((DEPTH, MLA_KV_LORA, MLA_HEADS, MLA_V), MLA_KV_LORA ** -0.5),
        'mla_qn_nope': gain((DEPTH, MLA_NOPE)),
        'mla_qn_rope': gain((DEPTH, MLA_ROPE // 2)),
        'mla_kn_nope': gain((DEPTH, MLA_NOPE)),
        'mla_kn_rope': gain((DEPTH, MLA_ROPE // 2)),
        'moba_qn': gain((DEPTH, MOBA_DH)),
        'moba_kn': gain((DEPTH, MOBA_DH)),
        'out_norm_ssm': gain((DEPTH, SSM_WIDTH)),
        'out_norm_mla': gain((DEPTH, MLA_WIDTH)),
        'out_norm_moba': gain((DEPTH, MOBA_WIDTH)),
        'w_out': nrm((DEPTH, D_MIX, D_MODEL), D_MIX ** -0.5),
        'norm_ffn': gain((DEPTH, D_MODEL)),
        'ffn_w_gate': nrm((N_DENSE, D_MODEL, D_FF), D_MODEL ** -0.5),
        'ffn_w_up': nrm((N_DENSE, D_MODEL, D_FF), D_MODEL ** -0.5),
        'ffn_w_down': nrm((N_DENSE, D_FF, D_MODEL), D_FF ** -0.5),
        'moe_w_router': nrm((N_MOE, D_MODEL, N_EXPERTS), D_MODEL ** -0.5),
        'moe_b_router': nrm((N_MOE, N_EXPERTS), 0.01),
        'moe_w_gate': nrm((N_MOE, N_EXPERTS, D_MODEL, D_FF_EXPERT), D_MODEL ** -0.5),
        'moe_w_up': nrm((N_MOE, N_EXPERTS, D_MODEL, D_FF_EXPERT), D_MODEL ** -0.5),
        'moe_w_down': nrm((N_MOE, N_EXPERTS, D_FF_EXPERT, D_MODEL), D_FF_EXPERT ** -0.5),
    }


def reference(x_prompt, x_sample, cache_mla_ckv, cache_mla_krope, cache_moba_k, cache_moba_v,
              state_ssm_re, state_ssm_im, page_table, norm_mix, w_in, ssm_lam_re, ssm_lam_im,
              ssm_log_dt, ssm_b_re, ssm_b_im, ssm_c_re, ssm_c_im, ssm_d, ssm_w_glu, ssm_b_glu,
              mla_q_a_norm, mla_kv_a_norm, mla_w_uq, mla_w_uk, mla_w_uv, mla_qn_nope, mla_qn_rope,
              mla_kn_nope, mla_kn_rope, moba_qn, moba_kn, out_norm_ssm, out_norm_mla, out_norm_moba,
              w_out, norm_ffn, ffn_w_gate, ffn_w_up, ffn_w_down, moe_w_router, moe_b_router,
              moe_w_gate, moe_w_up, moe_w_down):
    S = x_prompt.shape[1]
    L = x_sample.shape[1]
    cos_p, sin_p = _rope_cs(jnp.arange(S))
    cos_s, sin_s = _rope_cs(PAST_LEN + jnp.arange(L))
    h0_p = jnp.zeros((x_prompt.shape[0], SSM_GROUPS, SSM_STATE), jnp.complex64)
    yp, ys = x_prompt, x_sample
    st_p, st_s = [], []
    for l in range(DEPTH):
        lw = {
            'norm_mix': norm_mix[l], 'w_in': w_in[l],
            'lam_re': ssm_lam_re[l], 'lam_im': ssm_lam_im[l], 'log_dt': ssm_log_dt[l],
            'b_re': ssm_b_re[l], 'b_im': ssm_b_im[l], 'c_re': ssm_c_re[l], 'c_im': ssm_c_im[l],
            'd': ssm_d[l], 'w_glu': ssm_w_glu[l], 'b_glu': ssm_b_glu[l],
            'q_a_norm': mla_q_a_norm[l], 'kv_a_norm': mla_kv_a_norm[l], 'w_uq': mla_w_uq[l],
            'qn_nope': mla_qn_nope[l], 'qn_rope': mla_qn_rope[l],
            'moba_qn': moba_qn[l], 'moba_kn': moba_kn[l],
            'g_ssm': out_norm_ssm[l], 'g_mla': out_norm_mla[l], 'g_moba': out_norm_moba[l],
            'w_out': w_out[l],
        }
        mla_p = functools.partial(_mla_prompt_attend, w_uk=mla_w_uk[l], w_uv=mla_w_uv[l],
                                  g_nope=mla_kn_nope[l], g_rope=mla_kn_rope[l])
        mla_s = functools.partial(_mla_sample, l, cache_ckv=cache_mla_ckv, cache_kr=cache_mla_krope,
                                  page_table=page_table, w_uk=mla_w_uk[l], w_uv=mla_w_uv[l],
                                  g_nope=mla_kn_nope[l], g_rope=mla_kn_rope[l])
        moba_s = functools.partial(_moba_sample, l, cache_k=cache_moba_k, cache_v=cache_moba_v,
                                   page_table=page_table)
        h0_s = lax.complex(state_ssm_re[l].astype(F32), state_ssm_im[l].astype(F32))
        yp, sp = _mixer(yp, lw, cos_p, sin_p, h0_p, mla_p, _moba_prompt)
        ys, ss = _mixer(ys, lw, cos_s, sin_s, h0_s, mla_s, moba_s)
        j = l // 2
        hp = _rms(yp, norm_ffn[l]).reshape(-1, D_MODEL)
        hs = _rms(ys, norm_ffn[l]).reshape(-1, D_MODEL)
        if l % 2 == 0:
            fp = _swiglu(hp, ffn_w_gate[j], ffn_w_up[j], ffn_w_down[j])
            fs = _swiglu(hs, ffn_w_gate[j], ffn_w_up[j], ffn_w_down[j])
        else:
            fp = _moe(hp, moe_w_router[j], moe_b_router[j], moe_w_gate[j], moe_w_up[j], moe_w_down[j])
            fs = _moe(hs, moe_w_router[j], moe_b_router[j], moe_w_gate[j], moe_w_up[j], moe_w_down[j])
        yp = yp + fp.reshape(yp.shape)
        ys = ys + fs.reshape(ys.shape)
        st_p.append(sp)
        st_s.append(ss)
    sdt = state_ssm_re.dtype
    ckv_prompt = jnp.stack([s[0] for s in st_p])
    krope_prompt = jnp.stack([s[1] for s in st_p])
    mobak_prompt = jnp.stack([s[2] for s in st_p])
    mobav_prompt = jnp.stack([s[3] for s in st_p])
    ssm_re_prompt = jnp.stack([jnp.real(s[4]) for s in st_p]).astype(sdt)
    ssm_im_prompt = jnp.stack([jnp.imag(s[4]) for s in st_p]).astype(sdt)
    ckv_sample = jnp.stack([s[0] for s in st_s])
    krope_sample = jnp.stack([s[1] for s in st_s])
    mobak_sample = jnp.stack([s[2] for s in st_s])
    mobav_sample = jnp.stack([s[3] for s in st_s])
    ssm_re_sample = jnp.stack([jnp.real(s[4]) for s in st_s]).astype(sdt)
    ssm_im_sample = jnp.stack([jnp.imag(s[4]) for s in st_s]).astype(sdt)
    return (yp, ys, ckv_prompt, krope_prompt, mobak_prompt, mobav_prompt, ssm_re_prompt, ssm_im_prompt,
            ckv_sample, krope_sample, mobak_sample, mobav_sample, ssm_re_sample, ssm_im_sample)
```

```python
import functools
import math

import jax
import jax.numpy as jnp
import numpy as np
from jax import lax
from jax.experimental import pallas as pl
from jax.experimental.pallas import tpu as pltpu

F32 = jnp.float32
BF16 = jnp.bfloat16

NORM_EPS = 1e-6
ROPE_THETA = 10000.0
MOBA_BLOCK = 256
MOBA_TOPK = 3
MOE_TOP_K = 2
LANES = 128
SUBLANES = 8
VMEM_LIMIT_BYTES = 56 * 1024 * 1024
PAGES_PER_STEP = 8
NEG_INF = float("-inf")


def _cparams(*sem):
    return pltpu.CompilerParams(dimension_semantics=sem, vmem_limit_bytes=VMEM_LIMIT_BYTES)


def _dot(a, b):
    return jnp.dot(a, b, preferred_element_type=F32)


def _dot_nt(a, b):
    return lax.dot_general(a, b, (((1,), (1,)), ((), ())), preferred_element_type=F32)


def _split(a):
    hi = a.astype(BF16)
    lo = (a - hi.astype(F32)).astype(BF16)
    return hi, lo


def _dot_hi(ah, al, bh, bl):
    return _dot(ah, bh) + (_dot(ah, bl) + _dot(al, bh))


def _dot_nt_hi(ah, al, bh, bl):
    return _dot_nt(ah, bh) + (_dot_nt(ah, bl) + _dot_nt(al, bh))


def _rms(x, g):
    return x * lax.rsqrt(jnp.mean(x * x, axis=-1, keepdims=True) + NORM_EPS) * g


def _row_tile(cap, *sizes):
    t = cap
    while any(s % t for s in sizes):
        t //= 2
    return t


def _iota(shape, dim):
    return lax.broadcasted_iota(jnp.int32, shape, dim)


def _inproj_kernel(x_ref, gmix_ref, w_ref, cs_ref, sn_ref, gq_ref, gkv_ref, gmq_ref, gmk_ref,
                   u_ref, cq_ref, ckv_ref, krp_ref, kr_ref, qm_ref, km_ref, vm_ref,
                   *, ws, ql, kvl, rope, wm, dh):
    h = _rms(x_ref[...], gmix_ref[...]).astype(BF16)
    o = 0
    u_ref[...] = _dot(h, w_ref[:, o:o + ws])
    o += ws
    cq_ref[...] = _rms(_dot(h, w_ref[:, o:o + ql]), gq_ref[...]).astype(BF16)
    o += ql
    ckv_ref[...] = _rms(_dot(h, w_ref[:, o:o + kvl]), gkv_ref[...])
    o += kvl
    kab = _dot(h, w_ref[:, o:o + 2 * LANES])
    o += 2 * LANES
    kr = kab[:, :LANES] * cs_ref[...] + kab[:, LANES:] * sn_ref[...]
    krp_ref[...] = kr
    kr_ref[...] = kr[:, :rope]
    qm = _dot(h, w_ref[:, o:o + wm])
    o += wm
    for hh in range(wm // dh):
        qm_ref[:, hh * dh:(hh + 1) * dh] = _rms(qm[:, hh * dh:(hh + 1) * dh], gmq_ref[...])
    kv = _dot(h, w_ref[:, o:o + 2 * dh])
    km_ref[...] = _rms(kv[:, :dh], gmk_ref[...])
    vm_ref[...] = kv[:, dh:]


def _inproj(x, gmix, w, cs, sn, gq, gkv, gmq, gmk, *, tm, ws, ql, kvl, rope, wm, dh):
    T, D = x.shape
    NC = w.shape[1]
    row = lambda n: pl.BlockSpec((tm, n), lambda i: (i, 0))
    full = lambda a: pl.BlockSpec(a.shape, lambda i: (0,) * a.ndim)
    outs = [(ws, F32), (ql, BF16), (kvl, F32), (LANES, F32), (rope, F32), (wm, F32), (dh, F32), (dh, F32)]
    return pl.pallas_call(
        functools.partial(_inproj_kernel, ws=ws, ql=ql, kvl=kvl, rope=rope, wm=wm, dh=dh),
        grid=(T // tm,),
        in_specs=[row(D), full(gmix), full(w), row(LANES), row(LANES), full(gq), full(gkv), full(gmq), full(gmk)],
        out_specs=[row(n) for n, _ in outs],
        out_shape=[jax.ShapeDtypeStruct((T, n), dt) for n, dt in outs],
        compiler_params=_cparams("parallel"),
        name="mixer_in_proj",
    )(x, gmix, w, cs, sn, gq, gkv, gmq, gmk)


def _qproj_kernel(cq_ref, w_ref, cs_ref, sn_ref, gn_ref, gr_ref, q_ref, *, nope, qk):
    r = _dot(cq_ref[...], w_ref[0])
    qn = r[:, :nope]
    qr = r[:, nope:nope + LANES] * cs_ref[...] + r[:, nope + LANES:] * sn_ref[...]
    ssq = jnp.sum(qn * qn, axis=-1, keepdims=True) + jnp.sum(qr * qr, axis=-1, keepdims=True)
    rinv = lax.rsqrt(ssq / qk + NORM_EPS)
    q_ref[0, :, :nope] = (qn * rinv * gn_ref[...]).astype(BF16)
    q_ref[0, :, nope:] = (qr * rinv * gr_ref[...]).astype(BF16)


def _qproj(cq, w, cs, sn, gn, gr, *, tm, nope, qk):
    T, ql = cq.shape
    H = w.shape[0]
    return pl.pallas_call(
        functools.partial(_qproj_kernel, nope=nope, qk=qk),
        grid=(T // tm, H),
        in_specs=[pl.BlockSpec((tm, ql), lambda i, h: (i, 0)),
                  pl.BlockSpec((1,) + w.shape[1:], lambda i, h: (h, 0, 0)),
                  pl.BlockSpec((tm, LANES), lambda i, h: (i, 0)),
                  pl.BlockSpec((tm, LANES), lambda i, h: (i, 0)),
                  pl.BlockSpec(gn.shape, lambda i, h: (0, 0)),
                  pl.BlockSpec(gr.shape, lambda i, h: (0, 0))],
        out_specs=pl.BlockSpec((1, tm, nope + LANES), lambda i, h: (h, i, 0)),
        out_shape=jax.ShapeDtypeStruct((H, T, nope + LANES), BF16),
        compiler_params=_cparams("parallel", "arbitrary"),
        name="mla_q_proj",
    )(cq, w, cs, sn, gn, gr)


def _kvp_kernel(ckv_ref, krp_ref, w_ref, gn_ref, gr_ref, k_ref, v_ref, *, nope, qk):
    r = _dot(ckv_ref[...].astype(BF16), w_ref[0])
    kn = r[:, :nope]
    kr = krp_ref[...]
    ssq = jnp.sum(kn * kn, axis=-1, keepdims=True) + jnp.sum(kr * kr, axis=-1, keepdims=True)
    rinv = lax.rsqrt(ssq / qk + NORM_EPS)
    k_ref[0, :, :nope] = (kn * rinv * gn_ref[...]).astype(BF16)
    k_ref[0, :, nope:] = (kr * rinv * gr_ref[...]).astype(BF16)
    v_ref[0] = r[:, nope:].astype(BF16)


def _kv_prompt(ckv, krp, w, gn, gr, *, rows, tm, nope, qk):
    kvl = ckv.shape[1]
    H, _, nv = w.shape
    vd = nv - nope
    return pl.pallas_call(
        functools.partial(_kvp_kernel, nope=nope, qk=qk),
        grid=(rows // tm, H),
        in_specs=[pl.BlockSpec((tm, kvl), lambda i, h: (i, 0)),
                  pl.BlockSpec((tm, LANES), lambda i, h: (i, 0)),
                  pl.BlockSpec((1, kvl, nv), lambda i, h: (h, 0, 0)),
                  pl.BlockSpec(gn.shape, lambda i, h: (0, 0)),
                  pl.BlockSpec(gr.shape, lambda i, h: (0, 0))],
        out_specs=[pl.BlockSpec((1, tm, nope + LANES), lambda i, h: (h, i, 0)),
                   pl.BlockSpec((1, tm, vd), lambda i, h: (h, i, 0))],
        out_shape=[jax.ShapeDtypeStruct((H, rows, nope + LANES), BF16),
                   jax.ShapeDtypeStruct((H, rows, vd), BF16)],
        compiler_params=_cparams("parallel", "arbitrary"),
        name="mla_kv_prompt",
    )(ckv, krp, w, gn, gr)


def _flash_kernel(q_ref, k_ref, v_ref, o_ref, m_ref, l_ref, acc_ref, *, scale, tq):
    i = pl.program_id(2)
    j = pl.program_id(3)

    @pl.when(j == 0)
    def _():
        m_ref[...] = jnp.full(m_ref.shape, NEG_INF, F32)
        l_ref[...] = jnp.zeros(l_ref.shape, F32)
        acc_ref[...] = jnp.zeros(acc_ref.shape, F32)

    @pl.when(j <= i)
    def _():
        s = _dot_nt(q_ref[0], k_ref[0]) * scale
        row = _iota(s.shape, 0) + i * tq
        col = _iota(s.shape, 1) + j * tq
        s = jnp.where(col <= row, s, NEG_INF)
        m_prev = m_ref[...]
        m_new = jnp.maximum(m_prev, jnp.max(s, axis=-1, keepdims=True))
        alpha = jnp.exp(m_prev - m_new)
        p = jnp.exp(s - m_new)
        l_ref[...] = alpha * l_ref[...] + jnp.sum(p, axis=-1, keepdims=True)
        acc_ref[...] = alpha * acc_ref[...] + _dot(p.astype(BF16), v_ref[0])
        m_ref[...] = m_new

    @pl.when(j == i)
    def _():
        o_ref[...] = acc_ref[...] / l_ref[...]


def _flash_prompt(q, k, v, *, B, S, tq, qk):
    H, _, dq = q.shape
    vd = v.shape[2]
    nq = S // tq
    return pl.pallas_call(
        functools.partial(_flash_kernel, scale=qk ** -0.5, tq=tq),
        grid=(B, H, nq, nq),
        in_specs=[pl.BlockSpec((1, tq, dq), lambda b, h, i, j: (h, b * nq + i, 0)),
                  pl.BlockSpec((1, tq, dq), lambda b, h, i, j: (h, b * nq + jnp.minimum(j, i), 0)),
                  pl.BlockSpec((1, tq, vd), lambda b, h, i, j: (h, b * nq + jnp.minimum(j, i), 0))],
        out_specs=pl.BlockSpec((tq, vd), lambda b, h, i, j: (b * nq + i, h)),
        out_shape=jax.ShapeDtypeStruct((B * S, H * vd), F32),
        scratch_shapes=[pltpu.VMEM((tq, 1), F32), pltpu.VMEM((tq, 1), F32), pltpu.VMEM((tq, vd), F32)],
        compiler_params=_cparams("parallel", "parallel", "parallel", "arbitrary"),
        name="mla_prompt_attention",
    )(q, k, v)


def _absorb_kernel(q_ref, wuk_ref, gn_ref, gr_ref, qa_ref, qr_ref, *, nope, rope):
    q = q_ref[0].astype(F32)
    qn = (q[:, :nope] * gn_ref[...]).astype(BF16)
    qa_ref[0] = _dot_nt(qn, wuk_ref[0])
    qr_ref[0] = (q[:, nope:] * gr_ref[...])[:, :rope]


def _absorb(q, wuk, gn, gr, *, Tp, Ts, nope, rope):
    H, _, dq = q.shape
    kvl = wuk.shape[1]
    blk = Tp // Ts
    return pl.pallas_call(
        functools.partial(_absorb_kernel, nope=nope, rope=rope),
        grid=(H,),
        in_specs=[pl.BlockSpec((1, Ts, dq), lambda h: (h, blk, 0)),
                  pl.BlockSpec((1, kvl, nope), lambda h: (h, 0, 0)),
                  pl.BlockSpec(gn.shape, lambda h: (0, 0)),
                  pl.BlockSpec(gr.shape, lambda h: (0, 0))],
        out_specs=[pl.BlockSpec((1, Ts, kvl), lambda h: (h, 0, 0)),
                   pl.BlockSpec((1, Ts, rope), lambda h: (h, 0, 0))],
        out_shape=[jax.ShapeDtypeStruct((H, Ts, kvl), F32), jax.ShapeDtypeStruct((H, Ts, rope), F32)],
        compiler_params=_cparams("parallel"),
        name="mla_decode_absorb",
    )(q, wuk, gn, gr)


def _mla_dec_kernel(pt_ref, qa_ref, qr_ref, wukt_ref, wuv_ref, cnew_ref, krnew_ref, *rest,
                    pps, H, L, nope, qk, rope, vd):
    c_refs = rest[:pps]
    kr_refs = rest[pps:2 * pps]
    o_ref = rest[2 * pps]
    wext, qr16, m_ref, l_ref, acc_ref, cpad, krpad = rest[2 * pps + 1:]
    j = pl.program_id(1)
    HL = H * L
    scale = qk ** -0.5
    kvl = cpad.shape[1]

    def scores(c16, kr32):
        kt = _dot_nt(wext[...], c16)
        krsq = kr32 * kr32
        hi, lo = _split(krsq)
        ones = jnp.ones((L, rope), BF16)
        ksq = _dot_nt(ones, hi) + _dot_nt(ones, lo)
        sr = _dot_nt(qr16[...], kr32.astype(BF16))
        parts = []
        for h in range(H):
            kk = kt[h * nope:(h + 1) * nope]
            ssq = jnp.sum(kk * kk, axis=0, keepdims=True)
            rinv = lax.rsqrt((ssq + ksq) / qk + NORM_EPS)
            parts.append((kt[H * nope + h * L:H * nope + (h + 1) * L] + sr[h * L:(h + 1) * L]) * (rinv * scale))
        return jnp.concatenate(parts, axis=0)

    def update(s, c16):
        m_prev = m_ref[...]
        m_new = jnp.maximum(m_prev, jnp.max(s, axis=-1, keepdims=True))
        alpha = jnp.exp(m_prev - m_new)
        p = jnp.exp(s - m_new)
        l_ref[...] = alpha * l_ref[...] + jnp.sum(p, axis=-1, keepdims=True)
        acc_ref[...] = alpha * acc_ref[...] + _dot(p.astype(BF16), c16)
        m_ref[...] = m_new

    @pl.when(j == 0)
    def _():
        wext[:H * nope, :] = wukt_ref[...]
        wext[H * nope:, :] = qa_ref[...].reshape(HL, kvl).astype(BF16)
        qr16[...] = qr_ref[...].reshape(HL, rope).astype(BF16)
        m_ref[...] = jnp.full(m_ref.shape, NEG_INF, F32)
        l_ref[...] = jnp.zeros(l_ref.shape, F32)
        acc_ref[...] = jnp.zeros(acc_ref.shape, F32)
        cpad[...] = jnp.zeros(cpad.shape, F32)
        krpad[...] = jnp.zeros(krpad.shape, F32)
        cpad[:L, :] = cnew_ref[...]
        krpad[:L, :] = krnew_ref[...][:, :rope]
        c16 = cpad[...].astype(BF16)
        s = scores(c16, krpad[...])
        row = _iota(s.shape, 0)
        col = _iota(s.shape, 1)
        s = jnp.where(col <= (row % L), s, NEG_INF)
        update(s, c16)

    c16 = jnp.concatenate([r[...] for r in c_refs], axis=0).astype(BF16)
    kr32 = jnp.concatenate([r[...] for r in kr_refs], axis=0)
    update(scores(c16, kr32), c16)

    @pl.when(j == pl.num_programs(1) - 1)
    def _():
        o = (acc_ref[...] / l_ref[...]).astype(BF16)
        for h in range(H):
            o_ref[:, h * vd:(h + 1) * vd] = _dot(o[h * L:(h + 1) * L], wuv_ref[h])


def _mla_decode(layer, pt_flat, qa, qr, wukt, wuv, ckv_all, krp_all, cache_ckv, cache_kr,
                *, n, L, NP, Tp, nope, qk):
    H, Ts, kvl = qa.shape
    rope = qr.shape[2]
    vd = wuv.shape[2]
    page = cache_ckv.shape[2]
    pps = _row_tile(PAGES_PER_STEP, NP)
    nj = NP // pps
    HL = H * L
    off = Tp // L
    in_specs = [
        pl.BlockSpec((H, L, kvl), lambda i, j, pt: (0, i, 0)),
        pl.BlockSpec((H, L, rope), lambda i, j, pt: (0, i, 0)),
        pl.BlockSpec(wukt.shape, lambda i, j, pt: (0, 0)),
        pl.BlockSpec(wuv.shape, lambda i, j, pt: (0, 0, 0)),
        pl.BlockSpec((L, kvl), lambda i, j, pt: (off + i, 0)),
        pl.BlockSpec((L, LANES), lambda i, j, pt: (off + i, 0)),
    ]
    for p in range(pps):
        in_specs.append(pl.BlockSpec((None, None, page, kvl),
                                     lambda i, j, pt, p=p: (layer, pt[i * NP + j * pps + p], 0, 0)))
    for p in range(pps):
        in_specs.append(pl.BlockSpec((None, None, page, rope),
                                     lambda i, j, pt, p=p: (layer, pt[i * NP + j * pps + p], 0, 0)))
    return pl.pallas_call(
        functools.partial(_mla_dec_kernel, pps=pps, H=H, L=L, nope=nope, qk=qk, rope=rope, vd=vd),
        grid_spec=pltpu.PrefetchScalarGridSpec(
            num_scalar_prefetch=1, grid=(n, nj), in_specs=in_specs,
            out_specs=pl.BlockSpec((L, H * vd), lambda i, j, pt: (i, 0)),
            scratch_shapes=[pltpu.VMEM((H * nope + HL, kvl), BF16), pltpu.VMEM((HL, rope), BF16),
                            pltpu.VMEM((HL, 1), F32), pltpu.VMEM((HL, 1), F32), pltpu.VMEM((HL, kvl), F32),
                            pltpu.VMEM((LANES, kvl), F32), pltpu.VMEM((LANES, rope), F32)]),
        out_shape=jax.ShapeDtypeStruct((Ts, H * vd), F32),
        compiler_params=_cparams("parallel", "arbitrary"),
        name="mla_decode_attention",
    )(pt_flat, qa, qr, wukt, wuv, ckv_all, krp_all, *([cache_ckv] * pps), *([cache_kr] * pps))


def _alibi_slopes(nh):
    return [float(2.0 ** (-8.0 * (h + 1) / nh)) for h in range(nh)]


def _select_blocks(g, n_valid, nb):
    lane = _iota(g.shape, 1)
    gm = jnp.where(lane < n_valid, g, NEG_INF)
    ones = jnp.ones((g.shape[1], g.shape[1]), BF16)
    sel = jnp.zeros(g.shape, F32)
    for jj in range(nb):
        col = gm[:, jj:jj + 1]
        beats = (gm > col) | ((gm == col) & (lane < jj))
        cnt = _dot(beats.astype(BF16), ones)
        sel = jnp.where((lane == jj) & (cnt < MOBA_TOPK) & (lane < n_valid), 1.0, sel)
    return sel


def _kmean_kernel(k_ref, o_ref):
    o_ref[0] = jnp.mean(k_ref[...], axis=0, keepdims=True)


def _block_means(km, *, rows):
    dh = km.shape[1]
    nblk = rows // MOBA_BLOCK
    out = pl.pallas_call(
        _kmean_kernel,
        grid=(nblk,),
        in_specs=[pl.BlockSpec((MOBA_BLOCK, dh), lambda i: (i, 0))],
        out_specs=pl.BlockSpec((1, 1, dh), lambda i: (i, 0, 0)),
        out_shape=jax.ShapeDtypeStruct((nblk, 1, dh), F32),
        compiler_params=_cparams("parallel"),
        name="moba_block_means",
    )(km)
    return out.reshape(nblk, dh)


def _moba_prompt_kernel(q_ref, k_ref, v_ref, kmean_ref, o_ref, sel_ref, m_ref, l_ref, acc_ref,
                        *, nh, dh, nb):
    i = pl.program_id(1)
    j = pl.program_id(2)
    mb = MOBA_BLOCK
    scale = dh ** -0.5
    slopes = _alibi_slopes(nh)

    def update(h, s, v16):
        m_prev = m_ref[h]
        m_new = jnp.maximum(m_prev, jnp.max(s, axis=-1, keepdims=True))
        alpha = jnp.exp(m_prev - m_new)
        p = jnp.exp(s - m_new)
        l_ref[h] = alpha * l_ref[h] + jnp.sum(p, axis=-1, keepdims=True)
        acc_ref[h] = alpha * acc_ref[h] + _dot(p.astype(BF16), v16)
        m_ref[h] = m_new

    @pl.when(j == 0)
    def _():
        kmh, kml = _split(kmean_ref[0])
        k16 = k_ref[...].astype(BF16)
        v16 = v_ref[...].astype(BF16)
        row = _iota((mb, mb), 0)
        col = _iota((mb, mb), 1)
        dist = (row - col).astype(F32)
        for h in range(nh):
            q = q_ref[:, h * dh:(h + 1) * dh]
            qh, ql = _split(q)
            sel_ref[h] = _select_blocks(_dot_nt_hi(qh, ql, kmh, kml), i, nb)
            m_ref[h] = jnp.full((mb, 1), NEG_INF, F32)
            l_ref[h] = jnp.zeros((mb, 1), F32)
            acc_ref[h] = jnp.zeros((mb, dh), F32)
            s = _dot_nt(qh, k16) * scale - slopes[h] * dist
            update(h, jnp.where(col <= row, s, NEG_INF), v16)

    @pl.when((j > 0) & (j <= i))
    def _():
        kb = j - 1
        k16 = k_ref[...].astype(BF16)
        v16 = v_ref[...].astype(BF16)
        row = _iota((mb, mb), 0)
        col = _iota((mb, mb), 1)
        dist = (row - col).astype(F32) + ((i - kb) * mb).astype(F32)
        lane = _iota((mb, sel_ref.shape[2]), 1)
        for h in range(nh):
            picked = jnp.sum(jnp.where(lane == kb, sel_ref[h], 0.0), axis=-1, keepdims=True) > 0.5
            s = _dot_nt(q_ref[:, h * dh:(h + 1) * dh].astype(BF16), k16) * scale - slopes[h] * dist
            update(h, jnp.where(picked, s, NEG_INF), v16)

    @pl.when(j == pl.num_programs(2) - 1)
    def _():
        for h in range(nh):
            o_ref[:, h * dh:(h + 1) * dh] = acc_ref[h] / l_ref[h]


def _moba_prompt(qm, km, vm, kmean_pad, *, B, S, nh, dh):
    nb = S // MOBA_BLOCK
    nbp = kmean_pad.shape[1]
    mb = MOBA_BLOCK

    def kidx(b, i, j):
        return (b * nb + jnp.where(j == 0, i, jnp.minimum(j - 1, i)), 0)

    return pl.pallas_call(
        functools.partial(_moba_prompt_kernel, nh=nh, dh=dh, nb=nb),
        grid=(B, nb, nb),
        in_specs=[pl.BlockSpec((mb, nh * dh), lambda b, i, j: (b * nb + i, 0)),
                  pl.BlockSpec((mb, dh), kidx),
                  pl.BlockSpec((mb, dh), kidx),
                  pl.BlockSpec((1, nbp, dh), lambda b, i, j: (b, 0, 0))],
        out_specs=pl.BlockSpec((mb, nh * dh), lambda b, i, j: (b * nb + i, 0)),
        out_shape=jax.ShapeDtypeStruct((B * S, nh * dh), F32),
        scratch_shapes=[pltpu.VMEM((nh, mb, nbp), F32), pltpu.VMEM((nh, mb, 1), F32),
                        pltpu.VMEM((nh, mb, 1), F32), pltpu.VMEM((nh, mb, dh), F32)],
        compiler_params=_cparams("parallel", "parallel", "arbitrary"),
        name="moba_prompt_attention",
    )(qm, km, vm, kmean_pad)


def _moba_dec_select_kernel(pt_ref, q_ref, *rest, pps, nh, dh, nb, page):
    k_refs = rest[:pps]
    sel_ref = rest[pps]
    kmean = rest[pps + 1]
    j = pl.program_id(1)
    ppb = MOBA_BLOCK // page
    bps = pps // ppb
    nbp = kmean.shape[0]

    @pl.when(j == 0)
    def _():
        kmean[...] = jnp.zeros(kmean.shape, F32)

    for b in range(bps):
        tot = k_refs[b * ppb][...]
        for t in range(1, ppb):
            tot = tot + k_refs[b * ppb + t][...]
        kmean[pl.ds(j * bps + b, 1), :] = jnp.sum(tot, axis=0, keepdims=True) * (1.0 / MOBA_BLOCK)

    @pl.when(j == pl.num_programs(1) - 1)
    def _():
        kmh, kml = _split(kmean[...])
        for h in range(nh):
            qh, ql = _split(q_ref[:, h * dh:(h + 1) * dh])
            sel_ref[:, h * nbp:(h + 1) * nbp] = _select_blocks(_dot_nt_hi(qh, ql, kmh, kml), nb, nb)


def _moba_dec_attn_kernel(pt_ref, q_ref, sel_ref, knew_ref, vnew_ref, *rest, pps, nh, dh, L, page, past):
    k_refs = rest[:pps]
    v_refs = rest[pps:2 * pps]
    o_ref = rest[2 * pps]
    q16, selr, m_ref, l_ref, acc_ref, kpad, vpad = rest[2 * pps + 1:]
    j = pl.program_id(1)
    R = nh * L
    mb = MOBA_BLOCK
    ppb = mb // page
    bps = pps // ppb
    nbp = selr.shape[1]
    scale = dh ** -0.5
    slopes = _alibi_slopes(nh)
    rowi = _iota((R, 1), 0)
    slope = jnp.zeros((R, 1), F32)
    for h in range(nh):
        slope = jnp.where(rowi // L == h, slopes[h], slope)
    tpos = (past + rowi % L).astype(F32)

    def update(s, v16):
        m_prev = m_ref[...]
        m_new = jnp.maximum(m_prev, jnp.max(s, axis=-1, keepdims=True))
        alpha = jnp.exp(m_prev - m_new)
        p = jnp.exp(s - m_new)
        l_ref[...] = alpha * l_ref[...] + jnp.sum(p, axis=-1, keepdims=True)
        acc_ref[...] = alpha * acc_ref[...] + _dot(p.astype(BF16), v16)
        m_ref[...] = m_new

    @pl.when(j == 0)
    def _():
        for h in range(nh):
            q16[h * L:(h + 1) * L, :] = q_ref[:, h * dh:(h + 1) * dh]
            selr[h * L:(h + 1) * L, :] = sel_ref[:, h * nbp:(h + 1) * nbp]
        m_ref[...] = jnp.full(m_ref.shape, NEG_INF, F32)
        l_ref[...] = jnp.zeros(l_ref.shape, F32)
        acc_ref[...] = jnp.zeros(acc_ref.shape, F32)
        kpad[...] = jnp.zeros(kpad.shape, F32)
        vpad[...] = jnp.zeros(vpad.shape, F32)
        kpad[:L, :] = knew_ref[...]
        vpad[:L, :] = vnew_ref[...]
        s = _dot_nt(q16[...].astype(BF16), kpad[...].astype(BF16)) * scale
        col = _iota(s.shape, 1)
        s = s - slope * (tpos - (past + col).astype(F32))
        s = jnp.where(col <= (_iota(s.shape, 0) % L), s, NEG_INF)
        update(s, vpad[...].astype(BF16))

    qb = q16[...].astype(BF16)
    lane = _iota((R, nbp), 1)
    col = _iota((R, mb), 1)
    parts = []
    for b in range(bps):
        kb = j * bps + b
        k16 = jnp.concatenate([k_refs[b * ppb + t][...] for t in range(ppb)], axis=0).astype(BF16)
        picked = jnp.sum(jnp.where(lane == kb, selr[...], 0.0), axis=-1, keepdims=True) > 0.5
        pos = (kb * mb + col).astype(F32)
        s = _dot_nt(qb, k16) * scale - slope * (tpos - pos)
        parts.append(jnp.where(picked, s, NEG_INF))
    v16 = jnp.concatenate([r[...] for r in v_refs], axis=0).astype(BF16)
    update(jnp.concatenate(parts, axis=1), v16)

    @pl.when(j == pl.num_programs(1) - 1)
    def _():
        o = acc_ref[...] / l_ref[...]
        for h in range(nh):
            o_ref[:, h * dh:(h + 1) * dh] = o[h * L:(h + 1) * L]


def _moba_decode(layer, pt_flat, qm, km, vm, cache_k, cache_v, *, n, L, NP, Tp, nh, dh):
    page = cache_k.shape[2]
    past = NP * page
    assert past % MOBA_BLOCK == 0 and L <= MOBA_BLOCK and MOBA_BLOCK % page == 0
    nb = past // MOBA_BLOCK
    nbp = -(-nb // LANES) * LANES
    ppb = MOBA_BLOCK // page
    pps = max(_row_tile(PAGES_PER_STEP, NP), ppb)
    nj = NP // pps
    Ts = n * L
    off = Tp // L
    R = nh * L

    def page_spec(p):
        return pl.BlockSpec((None, None, page, dh),
                            lambda i, j, pt, p=p: (layer, pt[i * NP + j * pps + p], 0, 0))

    sel = pl.pallas_call(
        functools.partial(_moba_dec_select_kernel, pps=pps, nh=nh, dh=dh, nb=nb, page=page),
        grid_spec=pltpu.PrefetchScalarGridSpec(
            num_scalar_prefetch=1, grid=(n, nj),
            in_specs=[pl.BlockSpec((L, nh * dh), lambda i, j, pt: (off + i, 0))]
                     + [page_spec(p) for p in range(pps)],
            out_specs=pl.BlockSpec((L, nh * nbp), lambda i, j, pt: (i, 0)),
            scratch_shapes=[pltpu.VMEM((nbp, dh), F32)]),
        out_shape=jax.ShapeDtypeStruct((Ts, nh * nbp), F32),
        compiler_params=_cparams("parallel", "arbitrary"),
        name="moba_decode_select",
    )(pt_flat, qm, *([cache_k] * pps))

    return pl.pallas_call(
        functools.partial(_moba_dec_attn_kernel, pps=pps, nh=nh, dh=dh, L=L, page=page, past=past),
        grid_spec=pltpu.PrefetchScalarGridSpec(
            num_scalar_prefetch=1, grid=(n, nj),
            in_specs=[pl.BlockSpec((L, nh * dh), lambda i, j, pt: (off + i, 0)),
                      pl.BlockSpec((L, nh * nbp), lambda i, j, pt: (i, 0)),
                      pl.BlockSpec((L, dh), lambda i, j, pt: (off + i, 0)),
                      pl.BlockSpec((L, dh), lambda i, j, pt: (off + i, 0))]
                     + [page_spec(p) for p in range(pps)] + [page_spec(p) for p in range(pps)],
            out_specs=pl.BlockSpec((L, nh * dh), lambda i, j, pt: (i, 0)),
            scratch_shapes=[pltpu.VMEM((R, dh), F32), pltpu.VMEM((R, nbp), F32),
                            pltpu.VMEM((R, 1), F32), pltpu.VMEM((R, 1), F32), pltpu.VMEM((R, dh), F32),
                            pltpu.VMEM((LANES, dh), F32), pltpu.VMEM((LANES, dh), F32)]),
        out_shape=jax.ShapeDtypeStruct((Ts, nh * dh), F32),
        compiler_params=_cparams("parallel", "arbitrary"),
        name="moba_decode_attention",
    )(pt_flat, qm, sel, km, vm, *([cache_k] * pps), *([cache_v] * pps))


def _s5_kernel(u_ref, h0r_ref, h0i_ref, lamr_ref, lami_ref, brh_ref, brl_ref, bih_ref, bil_ref,
               cr_ref, ci_ref, d_ref, wglu_ref, bglu_ref, z_ref, hro_ref, hio_ref,
               re, im, hr, hi, *, S, Lc, ngb, kb, nbk):
    c = pl.program_id(0)

    @pl.when(c == 0)
    def _():
        hr[...] = h0r_ref[...]
        hi[...] = h0i_ref[...]

    u = u_ref[...]
    uh, ul = _split(u)
    for g in range(ngb):
        si = slice(g * kb, (g + 1) * kb)
        so = slice(g * nbk, (g + 1) * nbk)
        re[:, so] = _dot_hi(uh[:, si], ul[:, si], brh_ref[g], brl_ref[g])
        im[:, so] = _dot_hi(uh[:, si], ul[:, si], bih_ref[g], bil_ref[g])

    lamr = lamr_ref[...]
    lami = lami_ref[...]

    def step(t, carry):
        start = t * S
        if S % SUBLANES == 0:
            start = pl.multiple_of(start, SUBLANES)
        rows = pl.ds(start, S)
        pr = hr[...]
        pi = hi[...]
        nr = lamr * pr - lami * pi + re[rows, :]
        ni = lamr * pi + lami * pr + im[rows, :]
        hr[...] = nr
        hi[...] = ni
        re[rows, :] = nr
        im[rows, :] = ni
        return carry

    lax.fori_loop(0, Lc, step, 0)

    ys = []
    for g in range(ngb):
        si = slice(g * kb, (g + 1) * kb)
        so = slice(g * nbk, (g + 1) * nbk)
        ys.append(_dot(re[:, so].astype(BF16), cr_ref[g]) + _dot(im[:, so].astype(BF16), ci_ref[g])
                  + d_ref[:, si] * u[:, si])
    y = jnp.concatenate(ys, axis=1) if ngb > 1 else ys[0]
    z = jax.nn.gelu(y, approximate=True)
    z_ref[...] = z * jax.nn.sigmoid(_dot(z.astype(BF16), wglu_ref[...]) + bglu_ref[...])

    @pl.when(c == pl.num_programs(0) - 1)
    def _():
        hro_ref[...] = hr[...]
        hio_ref[...] = hi[...]


def _s5(u_tm, h0r, h0i, p, *, S, Lc):
    rows_total, W = u_tm.shape
    NS = h0r.shape[1]
    ngb, kb, nbk = p["brh"].shape
    rows = S * Lc
    full = lambda a: pl.BlockSpec(a.shape, lambda c: (0,) * a.ndim)
    ws = [p["lamr"], p["lami"], p["brh"], p["brl"], p["bih"], p["bil"], p["cr"], p["ci"],
          p["d"], p["wglu"], p["bglu"]]
    return pl.pallas_call(
        functools.partial(_s5_kernel, S=S, Lc=Lc, ngb=ngb, kb=kb, nbk=nbk),
        grid=(rows_total // rows,),
        in_specs=[pl.BlockSpec((rows, W), lambda c: (c, 0)), full(h0r), full(h0i)] + [full(a) for a in ws],
        out_specs=[pl.BlockSpec((rows, W), lambda c: (c, 0)), full(h0r), full(h0i)],
        out_shape=[jax.ShapeDtypeStruct((rows_total, W), F32),
                   jax.ShapeDtypeStruct(h0r.shape, F32), jax.ShapeDtypeStruct(h0r.shape, F32)],
        scratch_shapes=[pltpu.VMEM((rows, NS), F32), pltpu.VMEM((rows, NS), F32),
                        pltpu.VMEM((S, NS), F32), pltpu.VMEM((S, NS), F32)],
        compiler_params=_cparams("arbitrary"),
        name="s5_scan",
    )(u_tm, h0r, h0i, *ws)


def _outproj_kernel(x_ref, a_ref, b_ref, c_ref, ga_ref, gb_ref, gc_ref, w_ref, o_ref, *, wa, wb):
    na = _rms(a_ref[...], ga_ref[...]).astype(BF16)
    nb = _rms(b_ref[...], gb_ref[...]).astype(BF16)
    nc = _rms(c_ref[...], gc_ref[...]).astype(BF16)
    o_ref[...] = (x_ref[...] + _dot(na, w_ref[:wa, :]) + _dot(nb, w_ref[wa:wa + wb, :])
                  + _dot(nc, w_ref[wa + wb:, :]))


def _outproj(x, a, b, c, ga, gb, gc, w, *, tm):
    T, D = x.shape
    row = lambda arr: pl.BlockSpec((tm, arr.shape[1]), lambda i: (i, 0))
    full = lambda arr: pl.BlockSpec(arr.shape, lambda i: (0,) * arr.ndim)
    return pl.pallas_call(
        functools.partial(_outproj_kernel, wa=a.shape[1], wb=b.shape[1]),
        grid=(T // tm,),
        in_specs=[row(x), row(a), row(b), row(c), full(ga), full(gb), full(gc), full(w)],
        out_specs=row(x),
        out_shape=jax.ShapeDtypeStruct((T, D), F32),
        compiler_params=_cparams("parallel"),
        name="mixer_out_proj",
    )(x, a, b, c, ga, gb, gc, w)


def _ffn_kernel(x_ref, g_ref, wg_ref, wu_ref, wd_ref, o_ref, h_scr, acc):
    f = pl.program_id(1)

    @pl.when(f == 0)
    def _():
        h_scr[...] = _rms(x_ref[...], g_ref[...]).astype(BF16)
        acc[...] = jnp.zeros(acc.shape, F32)

    h = h_scr[...]
    a = _dot(h, wg_ref[...])
    b = _dot(h, wu_ref[...])
    acc[...] += _dot((jax.nn.silu(a) * b).astype(BF16), wd_ref[...])

    @pl.when(f == pl.num_programs(1) - 1)
    def _():
        o_ref[...] = x_ref[...] + acc[...]


def _ffn(x, g, wg, wu, wd, *, tm, tf):
    T, D = x.shape
    F = wg.shape[1]
    return pl.pallas_call(
        _ffn_kernel,
        grid=(T // tm, F // tf),
        in_specs=[pl.BlockSpec((tm, D), lambda i, f: (i, 0)),
                  pl.BlockSpec(g.shape, lambda i, f: (0, 0)),
                  pl.BlockSpec((D, tf), lambda i, f: (0, f)),
                  pl.BlockSpec((D, tf), lambda i, f: (0, f)),
                  pl.BlockSpec((tf, D), lambda i, f: (f, 0))],
        out_specs=pl.BlockSpec((tm, D), lambda i, f: (i, 0)),
        out_shape=jax.ShapeDtypeStruct((T, D), F32),
        scratch_shapes=[pltpu.VMEM((tm, D), BF16), pltpu.VMEM((tm, D), F32)],
        compiler_params=_cparams("parallel", "arbitrary"),
        name="ffn_swiglu",
    )(x, g, wg, wu, wd)


def _router_kernel(x_ref, g_ref, wrh_ref, wrl_ref, br_ref, meta_ref, cnt_ref, carry, *, E):
    i = pl.program_id(0)

    @pl.when(i == 0)
    def _():
        carry[...] = jnp.zeros(carry.shape, F32)

    h = _rms(x_ref[...], g_ref[...])
    hh, hl = _split(h)
    lg = _dot_hi(hh, hl, wrh_ref[...], wrl_ref[...]) + br_ref[...]
    tm = lg.shape[0]
    lane = _iota(lg.shape, 1)
    lanef = lane.astype(F32)
    lg = jnp.where(lane < E, lg, NEG_INF)
    m1 = jnp.max(lg, axis=-1, keepdims=True)
    e1 = jnp.min(jnp.where(lg == m1, lanef, float(LANES)), axis=-1, keepdims=True)
    lg2 = jnp.where(lanef == e1, NEG_INF, lg)
    m2 = jnp.max(lg2, axis=-1, keepdims=True)
    e2 = jnp.min(jnp.where(lg2 == m2, lanef, float(LANES)), axis=-1, keepdims=True)
    t = jnp.exp(m2 - m1)
    g1 = 1.0 / (1.0 + t)
    g2 = t / (1.0 + t)
    oh1 = lanef == e1
    oh2 = lanef == e2
    a = (oh1 | oh2).astype(F32)
    tri = (_iota((tm, tm), 0) > _iota((tm, tm), 1)).astype(BF16)
    before = _dot(tri, a.astype(BF16)) + carry[...]
    p1 = jnp.sum(jnp.where(oh1, before, 0.0), axis=-1, keepdims=True)
    p2 = jnp.sum(jnp.where(oh2, before, 0.0), axis=-1, keepdims=True)
    carry[...] = carry[...] + jnp.sum(a, axis=0, keepdims=True)
    meta = jnp.zeros(lg.shape, F32)
    for k, val in enumerate((e1, e2, g1, g2, p1, p2)):
        meta = jnp.where(lane == k, val, meta)
    meta_ref[...] = meta
    cnt_ref[...] = jnp.broadcast_to(carry[...], cnt_ref.shape)


def _router(x, g, wrh, wrl, br, *, tm, E):
    T, D = x.shape
    full = lambda a: pl.BlockSpec(a.shape, lambda i: (0,) * a.ndim)
    return pl.pallas_call(
        functools.partial(_router_kernel, E=E),
        grid=(T // tm,),
        in_specs=[pl.BlockSpec((tm, D), lambda i: (i, 0)), full(g), full(wrh), full(wrl), full(br)],
        out_specs=[pl.BlockSpec((tm, LANES), lambda i: (i, 0)),
                   pl.BlockSpec((SUBLANES, LANES), lambda i: (0, 0))],
        out_shape=[jax.ShapeDtypeStruct((T, LANES), F32), jax.ShapeDtypeStruct((SUBLANES, LANES), F32)],
        scratch_shapes=[pltpu.VMEM((1, LANES), F32)],
        compiler_params=_cparams("arbitrary"),
        name="moe_router",
    )(x, g, wrh, wrl, br)


def _row_copy(src, s, dst, d, sem):
    return pltpu.make_async_copy(src.at[pl.ds(s, 1)], dst.at[pl.ds(d, 1)], sem)


def _dispatch_kernel(dest_ref, x_ref, g_ref, xs_in, xs_out, hbuf, sem, *, tm):
    del xs_in
    i = pl.program_id(0)
    hbuf[...] = _rms(x_ref[...], g_ref[...])

    def issue(r, c):
        for k in range(MOE_TOP_K):
            _row_copy(hbuf, r, xs_out, dest_ref[(i * tm + r) * MOE_TOP_K + k], sem).start()
        return c

    lax.fori_loop(0, tm, issue, 0)

    def drain(r, c):
        for k in range(MOE_TOP_K):
            _row_copy(hbuf, 0, xs_out, 0, sem).wait()
        return c

    lax.fori_loop(0, tm, drain, 0)


def _dispatch(dest_flat, x, g, xs0, *, tm):
    T, D = x.shape
    return pl.pallas_call(
        functools.partial(_dispatch_kernel, tm=tm),
        grid_spec=pltpu.PrefetchScalarGridSpec(
            num_scalar_prefetch=1, grid=(T // tm,),
            in_specs=[pl.BlockSpec((tm, D), lambda i, d: (i, 0)),
                      pl.BlockSpec(g.shape, lambda i, d: (0, 0)),
                      pl.BlockSpec(memory_space=pl.ANY)],
            out_specs=pl.BlockSpec(memory_space=pl.ANY),
            scratch_shapes=[pltpu.VMEM((tm, D), F32), pltpu.SemaphoreType.DMA(())]),
        out_shape=jax.ShapeDtypeStruct(xs0.shape, xs0.dtype),
        input_output_aliases={3: 0},
        compiler_params=_cparams("arbitrary"),
        name="moe_dispatch",
    )(dest_flat, x, g, xs0)


def _expert_kernel(be_ref, na_ref, xs_ref, wg_ref, wu_ref, wd_ref, y_ref, acc):
    i = pl.program_id(0)
    f = pl.program_id(1)
    last = pl.num_programs(1) - 1
    active = i < na_ref[0]

    @pl.when(f == 0)
    def _():
        acc[...] = jnp.zeros(acc.shape, F32)

    @pl.when(active)
    def _():
        x = xs_ref[...].astype(BF16)
        a = _dot(x, wg_ref[0])
        b = _dot(x, wu_ref[0])
        acc[...] += _dot((jax.nn.silu(a) * b).astype(BF16), wd_ref[0])

    @pl.when(f == last)
    def _():
        y_ref[...] = acc[...]


def _experts(block_e, nact, xs, wg, wu, wd, *, bm, tf):
    NR, D = xs.shape
    E, _, F = wg.shape
    nf = F // tf

    def fidx(i, f, na):
        return jnp.where(i < na[0], f, nf - 1)

    return pl.pallas_call(
        _expert_kernel,
        grid_spec=pltpu.PrefetchScalarGridSpec(
            num_scalar_prefetch=2, grid=(NR // bm, nf),
            in_specs=[pl.BlockSpec((bm, D), lambda i, f, be, na: (jnp.minimum(i, jnp.maximum(na[0] - 1, 0)), 0)),
                      pl.BlockSpec((1, D, tf), lambda i, f, be, na: (be[i], 0, fidx(i, f, na))),
                      pl.BlockSpec((1, D, tf), lambda i, f, be, na: (be[i], 0, fidx(i, f, na))),
                      pl.BlockSpec((1, tf, D), lambda i, f, be, na: (be[i], fidx(i, f, na), 0))],
            out_specs=pl.BlockSpec((bm, D), lambda i, f, be, na: (i, 0)),
            scratch_shapes=[pltpu.VMEM((bm, D), F32)]),
        out_shape=jax.ShapeDtypeStruct((NR, D), F32),
        compiler_params=_cparams("arbitrary", "arbitrary"),
        name="moe_experts",
    )(block_e, nact, xs, wg, wu, wd)


def _combine_kernel(dest_ref, x_ref, meta_ref, ys_ref, o_ref, buf, sem, *, tm):
    i = pl.program_id(0)

    def issue(r, c):
        for k in range(MOE_TOP_K):
            _row_copy(ys_ref, dest_ref[(i * tm + r) * MOE_TOP_K + k], buf.at[k], r, sem).start()
        return c

    lax.fori_loop(0, tm, issue, 0)

    def drain(r, c):
        for k in range(MOE_TOP_K):
            _row_copy(ys_ref, 0, buf.at[k], 0, sem).wait()
        return c

    lax.fori_loop(0, tm, drain, 0)
    meta = meta_ref[...]
    o_ref[...] = x_ref[...] + meta[:, 2:3] * buf[0] + meta[:, 3:4] * buf[1]


def _combine(dest_flat, x, meta, ys, *, tm):
    T, D = x.shape
    return pl.pallas_call(
        functools.partial(_combine_kernel, tm=tm),
        grid_spec=pltpu.PrefetchScalarGridSpec(
            num_scalar_prefetch=1, grid=(T // tm,),
            in_specs=[pl.BlockSpec((tm, D), lambda i, d: (i, 0)),
                      pl.BlockSpec((tm, LANES), lambda i, d: (i, 0)),
                      pl.BlockSpec(memory_space=pl.ANY)],
            out_specs=pl.BlockSpec((tm, D), lambda i, d: (i, 0)),
            scratch_shapes=[pltpu.VMEM((MOE_TOP_K, tm, D), F32), pltpu.SemaphoreType.DMA(())]),
        out_shape=jax.ShapeDtypeStruct((T, D), F32),
        compiler_params=_cparams("arbitrary"),
        name="moe_combine",
    )(dest_flat, x, meta, ys)


def _moe(x, g, w_r, b_r, wg, wu, wd, *, tm):
    T, D = x.shape
    E = w_r.shape[1]
    bm = 512 if T * MOE_TOP_K >= 8 * 512 else 128
    tf = _row_tile(512, wg.shape[2])
    wr = jnp.zeros((D, LANES), F32).at[:, :E].set(w_r)
    wrh = wr.astype(BF16)
    wrl = (wr - wrh.astype(F32)).astype(BF16)
    br = jnp.zeros((1, LANES), F32).at[0, :E].set(b_r)
    meta, cnt = _router(x, g, wrh, wrl, br, tm=min(tm, 512), E=E)
    counts = cnt[0, :E].astype(jnp.int32)
    padded = (counts + bm - 1) // bm * bm
    ends = jnp.cumsum(padded)
    starts = ends - padded
    e_tok = meta[:, 0:MOE_TOP_K].astype(jnp.int32)
    pos = meta[:, 4:4 + MOE_TOP_K].astype(jnp.int32)
    dest = (starts[e_tok] + pos).reshape(-1)
    nblk = -(-(T * MOE_TOP_K) // bm) + E
    nact = (ends[-1] // bm).astype(jnp.int32).reshape(1)
    blk = jnp.arange(nblk, dtype=jnp.int32)
    be = jnp.minimum(jnp.searchsorted(ends, blk * bm, side="right"), E - 1).astype(jnp.int32)
    last_e = jnp.max(jnp.where(padded > 0, jnp.arange(E, dtype=jnp.int32), 0))
    be = jnp.where(blk < nact[0], be, last_e)
    xs = _dispatch(dest, x, g, jnp.zeros((nblk * bm, D), F32), tm=min(tm, 256))
    ys = _experts(be, nact, xs, wg, wu, wd, bm=bm, tf=tf)
    return _combine(dest, x, meta, ys, tm=min(tm, 256))


def _pad_cols(a, n):
    return jnp.pad(a, ((0, 0), (0, n - a.shape[1])))


def _rope_pair(w, rope):
    half = rope // 2
    return _pad_cols(w, LANES), _pad_cols(jnp.concatenate([w[:, half:], w[:, :half]], axis=1), LANES)


def _s5_params(lam_re, lam_im, log_dt, b_re, b_im, c_re, c_im, d, w_glu, b_glu):
    G, P, C = b_re.shape
    gpb = LANES // C
    ngb = G // gpb
    dt = jnp.exp(log_dt)[:, None]
    er = jnp.exp(lam_re * dt)
    lbr = er * jnp.cos(lam_im * dt)
    lbi = er * jnp.sin(lam_im * dt)
    den = lam_re * lam_re + lam_im * lam_im
    cr = ((lbr - 1.0) * lam_re + lbi * lam_im) / den
    ci = (lbi * lam_re - (lbr - 1.0) * lam_im) / den
    bbr = cr[:, :, None] * b_re - ci[:, :, None] * b_im
    bbi = cr[:, :, None] * b_im + ci[:, :, None] * b_re
    eye = jnp.eye(gpb, dtype=F32)

    def bdiag_in(b):
        return jnp.einsum("napc,ab->nacbp", b.reshape(ngb, gpb, P, C), eye).reshape(ngb, gpb * C, gpb * P)

    def bdiag_out(c):
        return jnp.einsum("nacp,ab->napbc", c.reshape(ngb, gpb, C, P), eye).reshape(ngb, gpb * P, gpb * C)

    br, bi = bdiag_in(bbr), bdiag_in(bbi)
    brh, bih = br.astype(BF16), bi.astype(BF16)
    return dict(lamr=lbr.reshape(1, G * P), lami=lbi.reshape(1, G * P),
                brh=brh, brl=(br - brh.astype(F32)).astype(BF16),
                bih=bih, bil=(bi - bih.astype(F32)).astype(BF16),
                cr=bdiag_out(c_re).astype(BF16), ci=bdiag_out(-c_im).astype(BF16),
                d=d.reshape(1, -1), wglu=w_glu.astype(BF16), bglu=b_glu.reshape(1, -1))


def kernel(x_prompt, x_sample, cache_mla_ckv, cache_mla_krope, cache_moba_k, cache_moba_v,
           state_ssm_re, state_ssm_im, page_table, norm_mix, w_in, ssm_lam_re, ssm_lam_im,
           ssm_log_dt, ssm_b_re, ssm_b_im, ssm_c_re, ssm_c_im, ssm_d, ssm_w_glu, ssm_b_glu,
           mla_q_a_norm, mla_kv_a_norm, mla_w_uq, mla_w_uk, mla_w_uv, mla_qn_nope, mla_qn_rope,
           mla_kn_nope, mla_kn_rope, moba_qn, moba_kn, out_norm_ssm, out_norm_mla, out_norm_moba,
           w_out, norm_ffn, ffn_w_gate, ffn_w_up, ffn_w_down, moe_w_router, moe_b_router,
           moe_w_gate, moe_w_up, moe_w_down):
    B, S, D = x_prompt.shape
    n, L, _ = x_sample.shape
    depth = norm_mix.shape[0]
    NP = page_table.shape[1]
    page = cache_mla_ckv.shape[2]
    past = NP * page
    kvl = cache_mla_ckv.shape[3]
    rope = cache_mla_krope.shape[3]
    dh = cache_moba_k.shape[3]
    G, P = ssm_lam_re.shape[1:]
    ws = ssm_d.shape[1]
    ql = mla_q_a_norm.shape[1]
    H, nope = mla_w_uk.shape[2:]
    vd = mla_w_uv.shape[3]
    qk = nope + rope
    wm = out_norm_moba.shape[1]
    nh = wm // dh
    Tp, Ts = B * S, n * L
    T = Tp + Ts
    assert dh == LANES and rope <= LANES and nope % LANES == 0 and L % SUBLANES == 0
    assert S % MOBA_BLOCK == 0 and Tp % Ts == 0
    tm = _row_tile(512, Tp, Ts)

    x = jnp.concatenate([x_prompt.reshape(Tp, D), x_sample.reshape(Ts, D)], axis=0)
    pos = jnp.concatenate([jnp.tile(jnp.arange(S), B), jnp.tile(past + jnp.arange(L), n)]).astype(F32)
    inv = ROPE_THETA ** (-jnp.arange(0, rope, 2, dtype=F32) / rope)
    ang = pos[:, None] * inv[None, :]
    cos, sin = jnp.cos(ang), jnp.sin(ang)
    cs = _pad_cols(jnp.concatenate([cos, cos], axis=1), LANES)
    sn = _pad_cols(jnp.concatenate([-sin, sin], axis=1), LANES)
    pt_flat = page_table.reshape(-1).astype(jnp.int32)
    row1 = lambda a: a.reshape(1, -1)
    rope_gain = lambda g: _pad_cols(row1(jnp.concatenate([g, g])), LANES)
    splits = np.cumsum([ws, ql, kvl, rope, wm, dh]).tolist()

    Bp = -(-B // SUBLANES) * SUBLANES
    zeros_state = jnp.zeros((Bp, G * P), F32)
    leaves = {k: [] for k in ("ckv_p", "kr_p", "km_p", "vm_p", "sr_p", "si_p",
                              "ckv_s", "kr_s", "km_s", "vm_s", "sr_s", "si_s")}
    for l in range(depth):
        wu_, wcq, wckv, wkr, wqm, wkm, wvm = jnp.split(w_in[l], splits, axis=1)
        kra, krb = _rope_pair(wkr, rope)
        w1 = jnp.concatenate([wu_, wcq, wckv, kra, krb, wqm, wkm, wvm], axis=1).astype(BF16)
        u, cq, ckv, krp, kr, qm, km, vm = _inproj(
            x, row1(norm_mix[l]), w1, cs, sn, row1(mla_q_a_norm[l]), row1(mla_kv_a_norm[l]),
            row1(moba_qn[l]), row1(moba_kn[l]), tm=tm, ws=ws, ql=ql, kvl=kvl, rope=rope, wm=wm, dh=dh)

        sp = _s5_params(ssm_lam_re[l], ssm_lam_im[l], ssm_log_dt[l], ssm_b_re[l], ssm_b_im[l],
                        ssm_c_re[l], ssm_c_im[l], ssm_d[l], ssm_w_glu[l], ssm_b_glu[l])
        up = jnp.pad(u[:Tp].reshape(B, S, ws).transpose(1, 0, 2), ((0, 0), (0, Bp - B), (0, 0)))
        zp, srp, sip = _s5(up.reshape(S * Bp, ws), zeros_state, zeros_state, sp, S=Bp, Lc=_row_tile(128, S))
        zp = zp.reshape(S, Bp, ws)[:, :B].transpose(1, 0, 2).reshape(Tp, ws)
        srp, sip = srp[:B], sip[:B]
        us = u[Tp:].reshape(n, L, ws).transpose(1, 0, 2).reshape(Ts, ws)
        zs, srs, sis = _s5(us, state_ssm_re[l].reshape(n, G * P), state_ssm_im[l].reshape(n, G * P),
                           sp, S=n, Lc=L)
        zs = zs.reshape(L, n, ws).transpose(1, 0, 2).reshape(Ts, ws)
        ssm_out = jnp.concatenate([zp, zs], axis=0)

        wq = mla_w_uq[l].reshape(ql, H, qk).transpose(1, 0, 2)
        qa_, qb_ = jax.vmap(lambda w: _rope_pair(w, rope))(wq[:, :, nope:])
        wq2 = jnp.concatenate([wq[:, :, :nope], qa_, qb_], axis=2).astype(BF16)
        q = _qproj(cq, wq2, cs, sn, row1(mla_qn_nope[l]), rope_gain(mla_qn_rope[l]), tm=tm, nope=nope, qk=qk)
        wuk = mla_w_uk[l].transpose(1, 0, 2)
        wuv = mla_w_uv[l].transpose(1, 0, 2)
        gkn, gkr = row1(mla_kn_nope[l]), rope_gain(mla_kn_rope[l])
        kp, vp = _kv_prompt(ckv, krp, jnp.concatenate([wuk, wuv], axis=2).astype(BF16), gkn, gkr,
                            rows=Tp, tm=tm, nope=nope, qk=qk)
        mla_p = _flash_prompt(q, kp, vp, B=B, S=S, tq=_row_tile(512, S), qk=qk)
        qa, qr = _absorb(q, wuk.astype(BF16), gkn, gkr, Tp=Tp, Ts=Ts, nope=nope, rope=rope)
        wukt = wuk.transpose(0, 2, 1).reshape(H * nope, kvl).astype(BF16)
        mla_s = _mla_decode(l, pt_flat, qa, qr, wukt, wuv.astype(BF16), ckv, krp, cache_mla_ckv,
                            cache_mla_krope, n=n, L=L, NP=NP, Tp=Tp, nope=nope, qk=qk)
        mla_out = jnp.concatenate([mla_p, mla_s], axis=0)

        nb = S // MOBA_BLOCK
        nbp = -(-nb // LANES) * LANES
        kmean = _block_means(km, rows=Tp).reshape(B, nb, dh)
        kmean = jnp.pad(kmean, ((0, 0), (0, nbp - nb), (0, 0)))
        moba_p = _moba_prompt(qm, km, vm, kmean, B=B, S=S, nh=nh, dh=dh)
        moba_s = _moba_decode(l, pt_flat, qm, km, vm, cache_moba_k, cache_moba_v,
                              n=n, L=L, NP=NP, Tp=Tp, nh=nh, dh=dh)
        moba_out = jnp.concatenate([moba_p, moba_s], axis=0)

        x = _outproj(x, ssm_out, mla_out, moba_out, row1(out_norm_ssm[l]), row1(out_norm_mla[l]),
                     row1(out_norm_moba[l]), w_out[l].astype(BF16), tm=tm)

        j = l // 2
        if l % 2 == 0:
            F = ffn_w_gate.shape[2]
            Fp = -(-F // 256) * 256
            wg = _pad_cols(ffn_w_gate[j], Fp).astype(BF16)
            wu = _pad_cols(ffn_w_up[j], Fp).astype(BF16)
            wd = jnp.pad(ffn_w_down[j], ((0, Fp - F), (0, 0))).astype(BF16)
            x = _ffn(x, row1(norm_ffn[l]), wg, wu, wd, tm=tm, tf=_row_tile(512, Fp))
        else:
            x = _moe(x, row1(norm_ffn[l]), moe_w_router[j], moe_b_router[j], moe_w_gate[j].astype(BF16),
                     moe_w_up[j].astype(BF16), moe_w_down[j].astype(BF16), tm=tm)

        leaves["ckv_p"].append(ckv[:Tp].reshape(B, S, kvl))
        leaves["kr_p"].append(kr[:Tp].reshape(B, S, rope))
        leaves["km_p"].append(km[:Tp].reshape(B, S, dh))
        leaves["vm_p"].append(vm[:Tp].reshape(B, S, dh))
        leaves["sr_p"].append(srp.reshape(B, G, P))
        leaves["si_p"].append(sip.reshape(B, G, P))
        leaves["ckv_s"].append(ckv[Tp:].reshape(n, L, kvl))
        leaves["kr_s"].append(kr[Tp:].reshape(n, L, rope))
        leaves["km_s"].append(km[Tp:].reshape(n, L, dh))
        leaves["vm_s"].append(vm[Tp:].reshape(n, L, dh))
        leaves["sr_s"].append(srs.reshape(n, G, P))
        leaves["si_s"].append(sis.reshape(n, G, P))

    st = lambda k: jnp.stack(leaves[k])
    sdt = state_ssm_re.dtype
    return (x[:Tp].reshape(B, S, D), x[Tp:].reshape(n, L, D),
            st("ckv_p"), st("kr_p"), st("km_p"), st("vm_p"), st("sr_p").astype(sdt), st("si_p").astype(sdt),
            st("ckv_s"), st("kr_s"), st("km_s"), st("vm_s"), st("sr_s").astype(sdt), st("si_s").astype(sdt))
```

```python
import functools
import math

import jax
import jax.numpy as jnp
import numpy as np
from jax import lax
from jax.experimental import pallas as pl
from jax.experimental.pallas import tpu as pltpu

F32 = jnp.float32
BF16 = jnp.bfloat16

NORM_EPS = 1e-6
ROPE_THETA = 10000.0
MOBA_BLOCK = 256
MOBA_TOPK = 3
MOE_TOP_K = 2
LANES = 128
SUBLANES = 8
VMEM_LIMIT_BYTES = 56 * 1024 * 1024
MLA_PAGES_PER_STEP = 16
MOBA_KEY_BLOCKS_PER_STEP = 4
MOBA_DECODE_CHUNK_BLOCKS = 8
NEG_INF = float("-inf")


def _cparams(*sem):
    return pltpu.CompilerParams(dimension_semantics=sem, vmem_limit_bytes=VMEM_LIMIT_BYTES)


def _dot(a, b):
    return jnp.dot(a, b, preferred_element_type=F32)


def _dot_nt(a, b):
    return lax.dot_general(a, b, (((1,), (1,)), ((), ())), preferred_element_type=F32)


def _split(a):
    hi = a.astype(BF16)
    lo = (a - hi.astype(F32)).astype(BF16)
    return hi, lo


def _dot_hi(ah, al, bh, bl):
    return _dot(ah, bh) + (_dot(ah, bl) + _dot(al, bh))


def _dot_nt_hi(ah, al, bh, bl):
    return _dot_nt(ah, bh) + (_dot_nt(ah, bl) + _dot_nt(al, bh))


def _rms(x, g):
    return x * lax.rsqrt(jnp.mean(x * x, axis=-1, keepdims=True) + NORM_EPS) * g


def _row_tile(cap, *sizes):
    t = cap
    while any(s % t for s in sizes):
        t //= 2
    return t


def _iota(shape, dim):
    return lax.broadcasted_iota(jnp.int32, shape, dim)


def _inproj_kernel(x_ref, gmix_ref, w_ref, cs_ref, sn_ref, gq_ref, gkv_ref, gmq_ref, gmk_ref,
                   u_ref, cq_ref, ckv_ref, krp_ref, kr_ref, qm_ref, km_ref, vm_ref,
                   *, ws, ql, kvl, rope, wm, dh):
    h = _rms(x_ref[...], gmix_ref[...]).astype(BF16)
    o = 0
    u_ref[...] = _dot(h, w_ref[:, o:o + ws])
    o += ws
    cq_ref[...] = _rms(_dot(h, w_ref[:, o:o + ql]), gq_ref[...]).astype(BF16)
    o += ql
    ckv_ref[...] = _rms(_dot(h, w_ref[:, o:o + kvl]), gkv_ref[...])
    o += kvl
    kab = _dot(h, w_ref[:, o:o + 2 * LANES])
    o += 2 * LANES
    kr = kab[:, :LANES] * cs_ref[...] + kab[:, LANES:] * sn_ref[...]
    krp_ref[...] = kr
    kr_ref[...] = kr[:, :rope]
    qm = _dot(h, w_ref[:, o:o + wm])
    o += wm
    for hh in range(wm // dh):
        qm_ref[:, hh * dh:(hh + 1) * dh] = _rms(qm[:, hh * dh:(hh + 1) * dh], gmq_ref[...])
    kv = _dot(h, w_ref[:, o:o + 2 * dh])
    km_ref[...] = _rms(kv[:, :dh], gmk_ref[...])
    vm_ref[...] = kv[:, dh:]


def _inproj(x, gmix, w, cs, sn, gq, gkv, gmq, gmk, *, tm, ws, ql, kvl, rope, wm, dh):
    T, D = x.shape
    NC = w.shape[1]
    row = lambda n: pl.BlockSpec((tm, n), lambda i: (i, 0))
    full = lambda a: pl.BlockSpec(a.shape, lambda i: (0,) * a.ndim)
    outs = [(ws, F32), (ql, BF16), (kvl, F32), (LANES, F32), (rope, F32), (wm, F32), (dh, F32), (dh, F32)]
    return pl.pallas_call(
        functools.partial(_inproj_kernel, ws=ws, ql=ql, kvl=kvl, rope=rope, wm=wm, dh=dh),
        grid=(T // tm,),
        in_specs=[row(D), full(gmix), full(w), row(LANES), row(LANES), full(gq), full(gkv), full(gmq), full(gmk)],
        out_specs=[row(n) for n, _ in outs],
        out_shape=[jax.ShapeDtypeStruct((T, n), dt) for n, dt in outs],
        compiler_params=_cparams("arbitrary"),
        name="mixer_in_proj",
    )(x, gmix, w, cs, sn, gq, gkv, gmq, gmk)


def _qproj_kernel(cq_ref, w_ref, cs_ref, sn_ref, gn_ref, gr_ref, q_ref, *, nope, qk):
    r = _dot(cq_ref[...], w_ref[0])
    qn = r[:, :nope]
    qr = r[:, nope:nope + LANES] * cs_ref[...] + r[:, nope + LANES:] * sn_ref[...]
    ssq = jnp.sum(qn * qn, axis=-1, keepdims=True) + jnp.sum(qr * qr, axis=-1, keepdims=True)
    rinv = lax.rsqrt(ssq / qk + NORM_EPS)
    q_ref[0, :, :nope] = (qn * rinv * gn_ref[...]).astype(BF16)
    q_ref[0, :, nope:] = (qr * rinv * gr_ref[...]).astype(BF16)


def _qproj(cq, w, cs, sn, gn, gr, *, tm, nope, qk):
    T, ql = cq.shape
    H = w.shape[0]
    return pl.pallas_call(
        functools.partial(_qproj_kernel, nope=nope, qk=qk),
        grid=(T // tm, H),
        in_specs=[pl.BlockSpec((tm, ql), lambda i, h: (i, 0)),
                  pl.BlockSpec((1,) + w.shape[1:], lambda i, h: (h, 0, 0)),
                  pl.BlockSpec((tm, LANES), lambda i, h: (i, 0)),
                  pl.BlockSpec((tm, LANES), lambda i, h: (i, 0)),
                  pl.BlockSpec(gn.shape, lambda i, h: (0, 0)),
                  pl.BlockSpec(gr.shape, lambda i, h: (0, 0))],
        out_specs=pl.BlockSpec((1, tm, nope + LANES), lambda i, h: (h, i, 0)),
        out_shape=jax.ShapeDtypeStruct((H, T, nope + LANES), BF16),
        compiler_params=_cparams("arbitrary", "arbitrary"),
        name="mla_q_proj",
    )(cq, w, cs, sn, gn, gr)


def _kvp_kernel(ckv_ref, krp_ref, w_ref, gn_ref, gr_ref, k_ref, v_ref, *, nope, qk):
    r = _dot(ckv_ref[...].astype(BF16), w_ref[0])
    kn = r[:, :nope]
    kr = krp_ref[...]
    ssq = jnp.sum(kn * kn, axis=-1, keepdims=True) + jnp.sum(kr * kr, axis=-1, keepdims=True)
    rinv = lax.rsqrt(ssq / qk + NORM_EPS)
    k_ref[0, :, :nope] = (kn * rinv * gn_ref[...]).astype(BF16)
    k_ref[0, :, nope:] = (kr * rinv * gr_ref[...]).astype(BF16)
    v_ref[0] = r[:, nope:].astype(BF16)


def _kv_prompt(ckv, krp, w, gn, gr, *, rows, tm, nope, qk):
    kvl = ckv.shape[1]
    H, _, nv = w.shape
    vd = nv - nope
    return pl.pallas_call(
        functools.partial(_kvp_kernel, nope=nope, qk=qk),
        grid=(rows // tm, H),
        in_specs=[pl.BlockSpec((tm, kvl), lambda i, h: (i, 0)),
                  pl.BlockSpec((tm, LANES), lambda i, h: (i, 0)),
                  pl.BlockSpec((1, kvl, nv), lambda i, h: (h, 0, 0)),
                  pl.BlockSpec(gn.shape, lambda i, h: (0, 0)),
                  pl.BlockSpec(gr.shape, lambda i, h: (0, 0))],
        out_specs=[pl.BlockSpec((1, tm, nope + LANES), lambda i, h: (h, i, 0)),
                   pl.BlockSpec((1, tm, vd), lambda i, h: (h, i, 0))],
        out_shape=[jax.ShapeDtypeStruct((H, rows, nope + LANES), BF16),
                   jax.ShapeDtypeStruct((H, rows, vd), BF16)],
        compiler_params=_cparams("arbitrary", "arbitrary"),
        name="mla_kv_prompt",
    )(ckv, krp, w, gn, gr)


def _flash_kernel(q_ref, k_ref, v_ref, o_ref, m_ref, l_ref, acc_ref, *, scale):
    i = pl.program_id(2)
    j = pl.program_id(3)

    @pl.when(j == 0)
    def _():
        m_ref[...] = jnp.full(m_ref.shape, NEG_INF, F32)
        l_ref[...] = jnp.zeros(l_ref.shape, F32)
        acc_ref[...] = jnp.zeros(acc_ref.shape, F32)

    def update(diagonal):
        s = _dot_nt(q_ref[0], k_ref[0]) * scale
        if diagonal:
            s = jnp.where(_iota(s.shape, 1) <= _iota(s.shape, 0), s, NEG_INF)
        m_prev = m_ref[...]
        m_new = jnp.maximum(m_prev, jnp.max(s, axis=-1, keepdims=True))
        alpha = jnp.exp(m_prev - m_new)
        p = jnp.exp(s - m_new)
        l_ref[...] = alpha * l_ref[...] + jnp.sum(p, axis=-1, keepdims=True)
        acc_ref[...] = alpha * acc_ref[...] + _dot(p.astype(BF16), v_ref[0])
        m_ref[...] = m_new

    @pl.when(j < i)
    def _():
        update(False)

    @pl.when(j == i)
    def _():
        update(True)
        o_ref[...] = acc_ref[...] / l_ref[...]


def _flash_prompt(q, k, v, *, B, S, tq, qk):
    H, _, dq = q.shape
    vd = v.shape[2]
    nq = S // tq
    return pl.pallas_call(
        functools.partial(_flash_kernel, scale=qk ** -0.5),
        grid=(B, H, nq, nq),
        in_specs=[pl.BlockSpec((1, tq, dq), lambda b, h, i, j: (h, b * nq + i, 0)),
                  pl.BlockSpec((1, tq, dq), lambda b, h, i, j: (h, b * nq + jnp.minimum(j, i), 0)),
                  pl.BlockSpec((1, tq, vd), lambda b, h, i, j: (h, b * nq + jnp.minimum(j, i), 0))],
        out_specs=pl.BlockSpec((tq, vd), lambda b, h, i, j: (b * nq + i, h)),
        out_shape=jax.ShapeDtypeStruct((B * S, H * vd), F32),
        scratch_shapes=[pltpu.VMEM((tq, 1), F32), pltpu.VMEM((tq, 1), F32), pltpu.VMEM((tq, vd), F32)],
        compiler_params=_cparams("arbitrary", "arbitrary", "arbitrary", "arbitrary"),
        name="mla_prompt_attention",
    )(q, k, v)


def _absorb_kernel(q_ref, wuk_ref, gn_ref, gr_ref, qa_ref, qr_ref, *, nope, rope):
    q = q_ref[0].astype(F32)
    qn = (q[:, :nope] * gn_ref[...]).astype(BF16)
    qa_ref[0] = _dot_nt(qn, wuk_ref[0])
    qr_ref[0] = (q[:, nope:] * gr_ref[...])[:, :rope]


def _absorb(q, wuk, gn, gr, *, Tp, Ts, nope, rope):
    H, _, dq = q.shape
    kvl = wuk.shape[1]
    blk = Tp // Ts
    return pl.pallas_call(
        functools.partial(_absorb_kernel, nope=nope, rope=rope),
        grid=(H,),
        in_specs=[pl.BlockSpec((1, Ts, dq), lambda h: (h, blk, 0)),
                  pl.BlockSpec((1, kvl, nope), lambda h: (h, 0, 0)),
                  pl.BlockSpec(gn.shape, lambda h: (0, 0)),
                  pl.BlockSpec(gr.shape, lambda h: (0, 0))],
        out_specs=[pl.BlockSpec((1, Ts, kvl), lambda h: (h, 0, 0)),
                   pl.BlockSpec((1, Ts, rope), lambda h: (h, 0, 0))],
        out_shape=[jax.ShapeDtypeStruct((H, Ts, kvl), F32), jax.ShapeDtypeStruct((H, Ts, rope), F32)],
        compiler_params=_cparams("arbitrary"),
        name="mla_decode_absorb",
    )(q, wuk, gn, gr)


def _mla_dec_kernel(pt_ref, qa_ref, qr_ref, wukt_ref, wuv_ref, cnew_ref, krnew_ref, *rest,
                    pps, H, L, nope, qk, rope, vd):
    c_refs = rest[:pps]
    kr_refs = rest[pps:2 * pps]
    o_ref = rest[2 * pps]
    wext, qr16, m_ref, l_ref, acc_ref, cpad, krpad = rest[2 * pps + 1:]
    j = pl.program_id(1)
    HL = H * L
    scale = qk ** -0.5
    kvl = cpad.shape[1]

    def scores(c16, krt):
        kt = _dot_nt(wext[...], c16)
        ksq = jnp.sum(krt * krt, axis=0, keepdims=True)
        sr = _dot(qr16[...], krt.astype(BF16))
        parts = []
        for h in range(H):
            kk = kt[h * nope:(h + 1) * nope]
            ssq = jnp.sum(kk * kk, axis=0, keepdims=True)
            rinv = lax.rsqrt((ssq + ksq) / qk + NORM_EPS)
            parts.append((kt[H * nope + h * L:H * nope + (h + 1) * L] + sr[h * L:(h + 1) * L]) * (rinv * scale))
        return jnp.concatenate(parts, axis=0)

    def update(s, c16):
        m_prev = m_ref[...]
        m_new = jnp.maximum(m_prev, jnp.max(s, axis=-1, keepdims=True))
        alpha = jnp.exp(m_prev - m_new)
        p = jnp.exp(s - m_new)
        l_ref[...] = alpha * l_ref[...] + jnp.sum(p, axis=-1, keepdims=True)
        acc_ref[...] = alpha * acc_ref[...] + _dot(p.astype(BF16), c16)
        m_ref[...] = m_new

    @pl.when(j == 0)
    def _():
        wext[:H * nope, :] = wukt_ref[...]
        wext[H * nope:, :] = qa_ref[...].reshape(HL, kvl).astype(BF16)
        qr16[...] = qr_ref[...].reshape(HL, rope).astype(BF16)
        m_ref[...] = jnp.full(m_ref.shape, NEG_INF, F32)
        l_ref[...] = jnp.zeros(l_ref.shape, F32)
        acc_ref[...] = jnp.zeros(acc_ref.shape, F32)
        cpad[...] = jnp.zeros(cpad.shape, F32)
        krpad[...] = jnp.zeros(krpad.shape, F32)
        cpad[:L, :] = cnew_ref[...]
        krpad[:L, :] = krnew_ref[...]
        c16 = cpad[...].astype(BF16)
        s = scores(c16, krpad[...].T[:rope])
        row = _iota(s.shape, 0)
        col = _iota(s.shape, 1)
        s = jnp.where(col <= (row % L), s, NEG_INF)
        update(s, c16)

    c16 = jnp.concatenate([r[...] for r in c_refs], axis=0).astype(BF16)
    krt = jnp.concatenate([r[...] for r in kr_refs], axis=1)
    update(scores(c16, krt), c16)

    @pl.when(j == pl.num_programs(1) - 1)
    def _():
        o = (acc_ref[...] / l_ref[...]).astype(BF16)
        for h in range(H):
            o_ref[:, h * vd:(h + 1) * vd] = _dot(o[h * L:(h + 1) * L], wuv_ref[h])


def _mla_decode(layer, pt_flat, qa, qr, wukt, wuv, ckv_all, krp_all, cache_ckv, cache_krt,
                *, n, L, NP, Tp, nope, qk):
    H, Ts, kvl = qa.shape
    rope = qr.shape[2]
    vd = wuv.shape[2]
    page = cache_ckv.shape[2]
    pps = _row_tile(MLA_PAGES_PER_STEP, NP)
    nj = NP // pps
    HL = H * L
    off = Tp // L
    in_specs = [
        pl.BlockSpec((H, L, kvl), lambda i, j, pt: (0, i, 0)),
        pl.BlockSpec((H, L, rope), lambda i, j, pt: (0, i, 0)),
        pl.BlockSpec(wukt.shape, lambda i, j, pt: (0, 0)),
        pl.BlockSpec(wuv.shape, lambda i, j, pt: (0, 0, 0)),
        pl.BlockSpec((L, kvl), lambda i, j, pt: (off + i, 0)),
        pl.BlockSpec((L, LANES), lambda i, j, pt: (off + i, 0)),
    ]
    for p in range(pps):
        in_specs.append(pl.BlockSpec((None, None, page, kvl),
                                     lambda i, j, pt, p=p: (layer, pt[i * NP + j * pps + p], 0, 0)))
    for p in range(pps):
        in_specs.append(pl.BlockSpec((None, None, rope, page),
                                     lambda i, j, pt, p=p: (layer, pt[i * NP + j * pps + p], 0, 0)))
    return pl.pallas_call(
        functools.partial(_mla_dec_kernel, pps=pps, H=H, L=L, nope=nope, qk=qk, rope=rope, vd=vd),
        grid_spec=pltpu.PrefetchScalarGridSpec(
            num_scalar_prefetch=1, grid=(n, nj), in_specs=in_specs,
            out_specs=pl.BlockSpec((L, H * vd), lambda i, j, pt: (i, 0)),
            scratch_shapes=[pltpu.VMEM((H * nope + HL, kvl), BF16), pltpu.VMEM((HL, rope), BF16),
                            pltpu.VMEM((HL, 1), F32), pltpu.VMEM((HL, 1), F32), pltpu.VMEM((HL, kvl), F32),
                            pltpu.VMEM((LANES, kvl), F32), pltpu.VMEM((LANES, LANES), F32)]),
        out_shape=jax.ShapeDtypeStruct((Ts, H * vd), F32),
        compiler_params=_cparams("arbitrary", "arbitrary"),
        name="mla_decode_attention",
    )(pt_flat, qa, qr, wukt, wuv, ckv_all, krp_all, *([cache_ckv] * pps), *([cache_krt] * pps))


def _alibi_slopes(nh):
    return [float(2.0 ** (-8.0 * (h + 1) / nh)) for h in range(nh)]


def _select_blocks(g, n_valid, nb):
    lane = _iota(g.shape, 1)
    gm = jnp.where(lane < n_valid, g, NEG_INF)
    ones = jnp.ones((g.shape[1], g.shape[1]), BF16)
    sel = jnp.zeros(g.shape, F32)
    for jj in range(nb):
        col = gm[:, jj:jj + 1]
        beats = (gm > col) | ((gm == col) & (lane < jj))
        cnt = _dot(beats.astype(BF16), ones)
        sel = jnp.where((lane == jj) & (cnt < MOBA_TOPK) & (lane < n_valid), 1.0, sel)
    return sel


def _kmean_kernel(k_ref, o_ref):
    o_ref[0] = jnp.mean(k_ref[...], axis=0, keepdims=True)


def _block_means(km, *, rows):
    dh = km.shape[1]
    nblk = rows // MOBA_BLOCK
    out = pl.pallas_call(
        _kmean_kernel,
        grid=(nblk,),
        in_specs=[pl.BlockSpec((MOBA_BLOCK, dh), lambda i: (i, 0))],
        out_specs=pl.BlockSpec((1, 1, dh), lambda i: (i, 0, 0)),
        out_shape=jax.ShapeDtypeStruct((nblk, 1, dh), F32),
        compiler_params=_cparams("arbitrary"),
        name="moba_block_means",
    )(km)
    return out.reshape(nblk, dh)


def _moba_prompt_kernel(q_ref, k_ref, v_ref, kg_ref, vg_ref, kmean_ref, o_ref, sel_ref, m_ref, l_ref, acc_ref,
                        *, nh, dh, nb, gk):
    i = pl.program_id(1)
    j = pl.program_id(2)
    mb = MOBA_BLOCK
    scale = dh ** -0.5
    slopes = _alibi_slopes(nh)

    def update(h, s, v16):
        m_prev = m_ref[h]
        m_new = jnp.maximum(m_prev, jnp.max(s, axis=-1, keepdims=True))
        alpha = jnp.exp(m_prev - m_new)
        p = jnp.exp(s - m_new)
        l_ref[h] = alpha * l_ref[h] + jnp.sum(p, axis=-1, keepdims=True)
        acc_ref[h] = alpha * acc_ref[h] + _dot(p.astype(BF16), v16)
        m_ref[h] = m_new

    @pl.when(j == 0)
    def _():
        kmh, kml = _split(kmean_ref[0])
        k16 = k_ref[...].astype(BF16)
        v16 = v_ref[...].astype(BF16)
        row = _iota((mb, mb), 0)
        col = _iota((mb, mb), 1)
        dist = (row - col).astype(F32)
        for h in range(nh):
            q = q_ref[:, h * dh:(h + 1) * dh]
            qh, ql = _split(q)
            sel_ref[h] = _select_blocks(_dot_nt_hi(qh, ql, kmh, kml), i, nb)
            m_ref[h] = jnp.full((mb, 1), NEG_INF, F32)
            l_ref[h] = jnp.zeros((mb, 1), F32)
            acc_ref[h] = jnp.zeros((mb, dh), F32)
            s = _dot_nt(qh, k16) * scale - slopes[h] * dist
            update(h, jnp.where(col <= row, s, NEG_INF), v16)

    @pl.when((j > 0) & ((j - 1) * gk < i))
    def _():
        kb0 = (j - 1) * gk
        k16 = kg_ref[...].astype(BF16)
        v16 = vg_ref[...].astype(BF16)
        row = _iota((mb, gk * mb), 0)
        col = _iota((mb, gk * mb), 1)
        dist = (row - col).astype(F32) + ((i - kb0) * mb).astype(F32)
        lane = _iota((mb, sel_ref.shape[2]), 1)
        for h in range(nh):
            sel = sel_ref[h]
            picked = [jnp.broadcast_to(jnp.sum(jnp.where(lane == kb0 + t, sel, 0.0), axis=-1, keepdims=True),
                                       (mb, mb)) for t in range(gk)]
            picked = jnp.concatenate(picked, axis=1) if gk > 1 else picked[0]
            s = _dot_nt(q_ref[:, h * dh:(h + 1) * dh].astype(BF16), k16) * scale - slopes[h] * dist
            update(h, jnp.where(picked > 0.5, s, NEG_INF), v16)

    @pl.when(j == pl.num_programs(2) - 1)
    def _():
        for h in range(nh):
            o_ref[:, h * dh:(h + 1) * dh] = acc_ref[h] / l_ref[h]


def _moba_prompt(qm, km, vm, kmean_pad, *, B, S, nh, dh):
    nb = S // MOBA_BLOCK
    nbp = kmean_pad.shape[1]
    mb = MOBA_BLOCK

    gk = _row_tile(MOBA_KEY_BLOCKS_PER_STEP, nb)
    ng = nb // gk

    def own(b, i, j):
        return (b * nb + i, 0)

    def group(b, i, j):
        last = jnp.maximum(i - 1, 0) // gk
        return (b * ng + jnp.minimum(jnp.maximum(j - 1, 0), last), 0)

    return pl.pallas_call(
        functools.partial(_moba_prompt_kernel, nh=nh, dh=dh, nb=nb, gk=gk),
        grid=(B, nb, 1 + ng),
        in_specs=[pl.BlockSpec((mb, nh * dh), own),
                  pl.BlockSpec((mb, dh), own),
                  pl.BlockSpec((mb, dh), own),
                  pl.BlockSpec((gk * mb, dh), group),
                  pl.BlockSpec((gk * mb, dh), group),
                  pl.BlockSpec((1, nbp, dh), lambda b, i, j: (b, 0, 0))],
        out_specs=pl.BlockSpec((mb, nh * dh), lambda b, i, j: (b * nb + i, 0)),
        out_shape=jax.ShapeDtypeStruct((B * S, nh * dh), F32),
        scratch_shapes=[pltpu.VMEM((nh, mb, nbp), F32), pltpu.VMEM((nh, mb, 1), F32),
                        pltpu.VMEM((nh, mb, 1), F32), pltpu.VMEM((nh, mb, dh), F32)],
        compiler_params=_cparams("arbitrary", "arbitrary", "arbitrary"),
        name="moba_prompt_attention",
    )(qm, km, vm, km, vm, kmean_pad)


def _moba_dec_kernel(pt_ref, q_ref, knew_ref, vnew_ref, kc_hbm, vc_hbm, o_ref,
                     kbuf, vbuf, kmean, s_scr, ksem, vsem, *, layer, n, NP, nh, dh, L, page, past, cb):
    i = pl.program_id(0)
    slot = i % 2
    mb = MOBA_BLOCK
    nb = past // mb
    nbp = kmean.shape[0]
    nchunk = nb // cb
    ct = cb * mb
    R = nh * L
    scale = dh ** -0.5
    slopes = _alibi_slopes(nh)

    def page_copy(hbm, buf, sem, seq, p, s):
        pg = pt_ref[seq * NP + p]
        return pltpu.make_async_copy(hbm.at[layer, pg], buf.at[s, pl.ds(p * page, page)], sem.at[s])

    def fetch(seq, s):
        def body(p, c):
            page_copy(kc_hbm, kbuf, ksem, seq, p, s).start()
            page_copy(vc_hbm, vbuf, vsem, seq, p, s).start()
            return c
        lax.fori_loop(0, NP, body, 0)

    def drain(hbm, buf, sem):
        def body(p, c):
            page_copy(hbm, buf, sem, i, p, slot).wait()
            return c
        lax.fori_loop(0, NP, body, 0)

    @pl.when(i == 0)
    def _():
        kmean[...] = jnp.zeros(kmean.shape, F32)
        fetch(0, 0)

    @pl.when(i + 1 < n)
    def _():
        fetch(i + 1, 1 - slot)

    rowi = _iota((R, 1), 0)
    slope = jnp.zeros((R, 1), F32)
    for h in range(nh):
        slope = jnp.where(rowi // L == h, slopes[h], slope)
    tpos = (past + rowi % L).astype(F32)
    q32 = jnp.concatenate([q_ref[:, h * dh:(h + 1) * dh] for h in range(nh)], axis=0)
    q16 = q32.astype(BF16)

    zpad = jnp.zeros((LANES - L, dh), F32)
    kpad16 = jnp.concatenate([knew_ref[...], zpad], axis=0).astype(BF16)
    vpad16 = jnp.concatenate([vnew_ref[...], zpad], axis=0).astype(BF16)
    s_new = _dot_nt(q16, kpad16) * scale
    coln = _iota(s_new.shape, 1)
    s_new = s_new - slope * (tpos - (past + coln).astype(F32))
    s_new = jnp.where(coln <= (_iota(s_new.shape, 0) % L), s_new, NEG_INF)

    drain(kc_hbm, kbuf, ksem)
    kk = kbuf.at[slot]

    def mean_body(b, c):
        blk = kk[pl.ds(pl.multiple_of(b * mb, mb), mb), :]
        kmean[pl.ds(b, 1), :] = jnp.sum(blk, axis=0, keepdims=True) * (1.0 / mb)
        return c

    lax.fori_loop(0, nb, mean_body, 0, unroll=4)
    kmh, kml = _split(kmean[...])
    qh, ql = _split(q32)
    sel = _select_blocks(_dot_nt_hi(qh, ql, kmh, kml), nb, nb)
    lane = _iota((R, nbp), 1)
    colc = _iota((R, ct), 1)

    def score_body(c, m_run):
        k16 = kk[pl.ds(pl.multiple_of(c * ct, ct), ct), :].astype(BF16)
        pos = (c * ct + colc).astype(F32)
        s = _dot_nt(q16, k16) * scale - slope * (tpos - pos)
        picked = [jnp.broadcast_to(jnp.sum(jnp.where(lane == c * cb + t, sel, 0.0), axis=-1, keepdims=True),
                                   (R, mb)) for t in range(cb)]
        picked = jnp.concatenate(picked, axis=1) if cb > 1 else picked[0]
        s = jnp.where(picked > 0.5, s, NEG_INF)
        s_scr[c] = s
        return jnp.maximum(m_run, jnp.max(s, axis=-1, keepdims=True))

    m = lax.fori_loop(0, nchunk, score_body, jnp.max(s_new, axis=-1, keepdims=True))
    p_new = jnp.exp(s_new - m)

    drain(vc_hbm, vbuf, vsem)
    vv = vbuf.at[slot]

    def pv_body(c, carry):
        l_run, acc = carry
        p = jnp.exp(s_scr[c] - m)
        v16 = vv[pl.ds(pl.multiple_of(c * ct, ct), ct), :].astype(BF16)
        return l_run + jnp.sum(p, axis=-1, keepdims=True), acc + _dot(p.astype(BF16), v16)

    l_run, acc = lax.fori_loop(0, nchunk, pv_body,
                               (jnp.sum(p_new, axis=-1, keepdims=True), _dot(p_new.astype(BF16), vpad16)))
    o = acc / l_run
    for h in range(nh):
        o_ref[:, h * dh:(h + 1) * dh] = o[h * L:(h + 1) * L]


def _moba_decode(layer, pt_flat, qm, km, vm, cache_k, cache_v, *, n, L, NP, Tp, nh, dh):
    page = cache_k.shape[2]
    past = NP * page
    assert past % MOBA_BLOCK == 0 and L <= MOBA_BLOCK and MOBA_BLOCK % page == 0
    nb = past // MOBA_BLOCK
    nbp = -(-nb // LANES) * LANES
    cb = _row_tile(MOBA_DECODE_CHUNK_BLOCKS, nb)
    Ts = n * L
    off = Tp // L
    R = nh * L
    return pl.pallas_call(
        functools.partial(_moba_dec_kernel, layer=layer, n=n, NP=NP, nh=nh, dh=dh, L=L, page=page,
                          past=past, cb=cb),
        grid_spec=pltpu.PrefetchScalarGridSpec(
            num_scalar_prefetch=1, grid=(n,),
            in_specs=[pl.BlockSpec((L, nh * dh), lambda i, pt: (off + i, 0)),
                      pl.BlockSpec((L, dh), lambda i, pt: (off + i, 0)),
                      pl.BlockSpec((L, dh), lambda i, pt: (off + i, 0)),
                      pl.BlockSpec(memory_space=pl.ANY),
                      pl.BlockSpec(memory_space=pl.ANY)],
            out_specs=pl.BlockSpec((L, nh * dh), lambda i, pt: (i, 0)),
            scratch_shapes=[pltpu.VMEM((2, past, dh), F32), pltpu.VMEM((2, past, dh), F32),
                            pltpu.VMEM((nbp, dh), F32), pltpu.VMEM((nb // cb, R, cb * MOBA_BLOCK), F32),
                            pltpu.SemaphoreType.DMA((2,)), pltpu.SemaphoreType.DMA((2,))]),
        out_shape=jax.ShapeDtypeStruct((Ts, nh * dh), F32),
        compiler_params=_cparams("arbitrary"),
        name="moba_decode_attention",
    )(pt_flat, qm, km, vm, cache_k, cache_v)


def _s5_kernel(u_ref, h0r_ref, h0i_ref, lamr_ref, lami_ref, brh_ref, brl_ref, bih_ref, bil_ref,
               cr_ref, ci_ref, d_ref, wglu_ref, bglu_ref, z_ref, hro_ref, hio_ref,
               re, im, hr, hi, *, S, Lc, ngb, kb, nbk):
    c = pl.program_id(0)

    @pl.when(c == 0)
    def _():
        hr[...] = h0r_ref[...]
        hi[...] = h0i_ref[...]

    u = u_ref[...]
    uh, ul = _split(u)
    for g in range(ngb):
        si = slice(g * kb, (g + 1) * kb)
        so = slice(g * nbk, (g + 1) * nbk)
        re[:, so] = _dot_hi(uh[:, si], ul[:, si], brh_ref[g], brl_ref[g])
        im[:, so] = _dot_hi(uh[:, si], ul[:, si], bih_ref[g], bil_ref[g])

    lamr = lamr_ref[...]
    lami = lami_ref[...]

    def step(t, carry):
        start = t * S
        if S % SUBLANES == 0:
            start = pl.multiple_of(start, SUBLANES)
        rows = pl.ds(start, S)
        pr = hr[...]
        pi = hi[...]
        nr = lamr * pr - lami * pi + re[rows, :]
        ni = lamr * pi + lami * pr + im[rows, :]
        hr[...] = nr
        hi[...] = ni
        re[rows, :] = nr
        im[rows, :] = ni
        return carry

    lax.fori_loop(0, Lc, step, 0)

    ys = []
    for g in range(ngb):
        si = slice(g * kb, (g + 1) * kb)
        so = slice(g * nbk, (g + 1) * nbk)
        ys.append(_dot(re[:, so].astype(BF16), cr_ref[g]) + _dot(im[:, so].astype(BF16), ci_ref[g])
                  + d_ref[:, si] * u[:, si])
    y = jnp.concatenate(ys, axis=1) if ngb > 1 else ys[0]
    z = jax.nn.gelu(y, approximate=True)
    z_ref[...] = z * jax.nn.sigmoid(_dot(z.astype(BF16), wglu_ref[...]) + bglu_ref[...])

    @pl.when(c == pl.num_programs(0) - 1)
    def _():
        hro_ref[...] = hr[...]
        hio_ref[...] = hi[...]


def _s5(u_tm, h0r, h0i, p, *, S, Lc):
    rows_total, W = u_tm.shape
    NS = h0r.shape[1]
    ngb, kb, nbk = p["brh"].shape
    rows = S * Lc
    full = lambda a: pl.BlockSpec(a.shape, lambda c: (0,) * a.ndim)
    ws = [p["lamr"], p["lami"], p["brh"], p["brl"], p["bih"], p["bil"], p["cr"], p["ci"],
          p["d"], p["wglu"], p["bglu"]]
    return pl.pallas_call(
        functools.partial(_s5_kernel, S=S, Lc=Lc, ngb=ngb, kb=kb, nbk=nbk),
        grid=(rows_total // rows,),
        in_specs=[pl.BlockSpec((rows, W), lambda c: (c, 0)), full(h0r), full(h0i)] + [full(a) for a in ws],
        out_specs=[pl.BlockSpec((rows, W), lambda c: (c, 0)), full(h0r), full(h0i)],
        out_shape=[jax.ShapeDtypeStruct((rows_total, W), F32),
                   jax.ShapeDtypeStruct(h0r.shape, F32), jax.ShapeDtypeStruct(h0r.shape, F32)],
        scratch_shapes=[pltpu.VMEM((rows, NS), F32), pltpu.VMEM((rows, NS), F32),
                        pltpu.VMEM((S, NS), F32), pltpu.VMEM((S, NS), F32)],
        compiler_params=_cparams("arbitrary"),
        name="s5_scan",
    )(u_tm, h0r, h0i, *ws)


def _outproj_kernel(x_ref, a_ref, b_ref, c_ref, ga_ref, gb_ref, gc_ref, w_ref, o_ref, *, wa, wb):
    na = _rms(a_ref[...], ga_ref[...]).astype(BF16)
    nb = _rms(b_ref[...], gb_ref[...]).astype(BF16)
    nc = _rms(c_ref[...], gc_ref[...]).astype(BF16)
    o_ref[...] = (x_ref[...] + _dot(na, w_ref[:wa, :]) + _dot(nb, w_ref[wa:wa + wb, :])
                  + _dot(nc, w_ref[wa + wb:, :]))


def _outproj(x, a, b, c, ga, gb, gc, w, *, tm):
    T, D = x.shape
    row = lambda arr: pl.BlockSpec((tm, arr.shape[1]), lambda i: (i, 0))
    full = lambda arr: pl.BlockSpec(arr.shape, lambda i: (0,) * arr.ndim)
    return pl.pallas_call(
        functools.partial(_outproj_kernel, wa=a.shape[1], wb=b.shape[1]),
        grid=(T // tm,),
        in_specs=[row(x), row(a), row(b), row(c), full(ga), full(gb), full(gc), full(w)],
        out_specs=row(x),
        out_shape=jax.ShapeDtypeStruct((T, D), F32),
        compiler_params=_cparams("arbitrary"),
        name="mixer_out_proj",
    )(x, a, b, c, ga, gb, gc, w)


def _ffn_kernel(x_ref, g_ref, wg_ref, wu_ref, wd_ref, o_ref, h_scr, acc):
    f = pl.program_id(1)

    @pl.when(f == 0)
    def _():
        h_scr[...] = _rms(x_ref[...], g_ref[...]).astype(BF16)
        acc[...] = jnp.zeros(acc.shape, F32)

    h = h_scr[...]
    a = _dot(h, wg_ref[...])
    b = _dot(h, wu_ref[...])
    acc[...] += _dot((jax.nn.silu(a) * b).astype(BF16), wd_ref[...])

    @pl.when(f == pl.num_programs(1) - 1)
    def _():
        o_ref[...] = x_ref[...] + acc[...]


def _ffn(x, g, wg, wu, wd, *, tm, tf):
    T, D = x.shape
    F = wg.shape[1]
    return pl.pallas_call(
        _ffn_kernel,
        grid=(T // tm, F // tf),
        in_specs=[pl.BlockSpec((tm, D), lambda i, f: (i, 0)),
                  pl.BlockSpec(g.shape, lambda i, f: (0, 0)),
                  pl.BlockSpec((D, tf), lambda i, f: (0, f)),
                  pl.BlockSpec((D, tf), lambda i, f: (0, f)),
                  pl.BlockSpec((tf, D), lambda i, f: (f, 0))],
        out_specs=pl.BlockSpec((tm, D), lambda i, f: (i, 0)),
        out_shape=jax.ShapeDtypeStruct((T, D), F32),
        scratch_shapes=[pltpu.VMEM((tm, D), BF16), pltpu.VMEM((tm, D), F32)],
        compiler_params=_cparams("arbitrary", "arbitrary"),
        name="ffn_swiglu",
    )(x, g, wg, wu, wd)


def _router_kernel(x_ref, g_ref, wrh_ref, wrl_ref, br_ref, meta_ref, cnt_ref, carry, *, E):
    i = pl.program_id(0)

    @pl.when(i == 0)
    def _():
        carry[...] = jnp.zeros(carry.shape, F32)

    h = _rms(x_ref[...], g_ref[...])
    hh, hl = _split(h)
    lg = _dot_hi(hh, hl, wrh_ref[...], wrl_ref[...]) + br_ref[...]
    tm = lg.shape[0]
    lane = _iota(lg.shape, 1)
    lanef = lane.astype(F32)
    lg = jnp.where(lane < E, lg, NEG_INF)
    m1 = jnp.max(lg, axis=-1, keepdims=True)
    e1 = jnp.min(jnp.where(lg == m1, lanef, float(LANES)), axis=-1, keepdims=True)
    lg2 = jnp.where(lanef == e1, NEG_INF, lg)
    m2 = jnp.max(lg2, axis=-1, keepdims=True)
    e2 = jnp.min(jnp.where(lg2 == m2, lanef, float(LANES)), axis=-1, keepdims=True)
    t = jnp.exp(m2 - m1)
    g1 = 1.0 / (1.0 + t)
    g2 = t / (1.0 + t)
    oh1 = lanef == e1
    oh2 = lanef == e2
    a = (oh1 | oh2).astype(F32)
    tri = (_iota((tm, tm), 0) > _iota((tm, tm), 1)).astype(BF16)
    before = _dot(tri, a.astype(BF16)) + carry[...]
    p1 = jnp.sum(jnp.where(oh1, before, 0.0), axis=-1, keepdims=True)
    p2 = jnp.sum(jnp.where(oh2, before, 0.0), axis=-1, keepdims=True)
    carry[...] = carry[...] + jnp.sum(a, axis=0, keepdims=True)
    meta = jnp.zeros(lg.shape, F32)
    for k, val in enumerate((e1, e2, g1, g2, p1, p2)):
        meta = jnp.where(lane == k, val, meta)
    meta_ref[...] = meta
    cnt_ref[...] = jnp.broadcast_to(carry[...], cnt_ref.shape)


def _router(x, g, wrh, wrl, br, *, tm, E):
    T, D = x.shape
    full = lambda a: pl.BlockSpec(a.shape, lambda i: (0,) * a.ndim)
    return pl.pallas_call(
        functools.partial(_router_kernel, E=E),
        grid=(T // tm,),
        in_specs=[pl.BlockSpec((tm, D), lambda i: (i, 0)), full(g), full(wrh), full(wrl), full(br)],
        out_specs=[pl.BlockSpec((tm, LANES), lambda i: (i, 0)),
                   pl.BlockSpec((SUBLANES, LANES), lambda i: (0, 0))],
        out_shape=[jax.ShapeDtypeStruct((T, LANES), F32), jax.ShapeDtypeStruct((SUBLANES, LANES), F32)],
        scratch_shapes=[pltpu.VMEM((1, LANES), F32)],
        compiler_params=_cparams("arbitrary"),
        name="moe_router",
    )(x, g, wrh, wrl, br)


def _row_copy(src, s, dst, d, sem):
    return pltpu.make_async_copy(src.at[pl.ds(s, 1)], dst.at[pl.ds(d, 1)], sem)


def _dispatch_kernel(dest_ref, x_ref, g_ref, xs_in, xs_out, hbuf, sem, *, tm):
    del xs_in
    i = pl.program_id(0)
    hbuf[...] = _rms(x_ref[...], g_ref[...])

    def issue(r, c):
        for k in range(MOE_TOP_K):
            _row_copy(hbuf, r, xs_out, dest_ref[(i * tm + r) * MOE_TOP_K + k], sem).start()
        return c

    lax.fori_loop(0, tm, issue, 0)

    def drain(r, c):
        for k in range(MOE_TOP_K):
            _row_copy(hbuf, 0, xs_out, 0, sem).wait()
        return c

    lax.fori_loop(0, tm, drain, 0)


def _dispatch(dest_flat, x, g, xs0, *, tm):
    T, D = x.shape
    return pl.pallas_call(
        functools.partial(_dispatch_kernel, tm=tm),
        grid_spec=pltpu.PrefetchScalarGridSpec(
            num_scalar_prefetch=1, grid=(T // tm,),
            in_specs=[pl.BlockSpec((tm, D), lambda i, d: (i, 0)),
                      pl.BlockSpec(g.shape, lambda i, d: (0, 0)),
                      pl.BlockSpec(memory_space=pl.ANY)],
            out_specs=pl.BlockSpec(memory_space=pl.ANY),
            scratch_shapes=[pltpu.VMEM((tm, D), F32), pltpu.SemaphoreType.DMA(())]),
        out_shape=jax.ShapeDtypeStruct(xs0.shape, xs0.dtype),
        input_output_aliases={3: 0},
        compiler_params=_cparams("arbitrary"),
        name="moe_dispatch",
    )(dest_flat, x, g, xs0)


def _expert_kernel(be_ref, na_ref, xs_ref, wg_ref, wu_ref, wd_ref, y_ref, acc):
    i = pl.program_id(0)
    f = pl.program_id(1)
    last = pl.num_programs(1) - 1
    active = i < na_ref[0]

    @pl.when(f == 0)
    def _():
        acc[...] = jnp.zeros(acc.shape, F32)

    @pl.when(active)
    def _():
        x = xs_ref[...].astype(BF16)
        a = _dot(x, wg_ref[0])
        b = _dot(x, wu_ref[0])
        acc[...] += _dot((jax.nn.silu(a) * b).astype(BF16), wd_ref[0])

    @pl.when(f == last)
    def _():
        y_ref[...] = acc[...]


def _experts(block_e, nact, xs, wg, wu, wd, *, bm, tf):
    NR, D = xs.shape
    E, _, F = wg.shape
    nf = F // tf

    def fidx(i, f, na):
        return jnp.where(i < na[0], f, nf - 1)

    return pl.pallas_call(
        _expert_kernel,
        grid_spec=pltpu.PrefetchScalarGridSpec(
            num_scalar_prefetch=2, grid=(NR // bm, nf),
            in_specs=[pl.BlockSpec((bm, D), lambda i, f, be, na: (jnp.minimum(i, jnp.maximum(na[0] - 1, 0)), 0)),
                      pl.BlockSpec((1, D, tf), lambda i, f, be, na: (be[i], 0, fidx(i, f, na))),
                      pl.BlockSpec((1, D, tf), lambda i, f, be, na: (be[i], 0, fidx(i, f, na))),
                      pl.BlockSpec((1, tf, D), lambda i, f, be, na: (be[i], fidx(i, f, na), 0))],
            out_specs=pl.BlockSpec((bm, D), lambda i, f, be, na: (i, 0)),
            scratch_shapes=[pltpu.VMEM((bm, D), F32)]),
        out_shape=jax.ShapeDtypeStruct((NR, D), F32),
        compiler_params=_cparams("arbitrary", "arbitrary"),
        name="moe_experts",
    )(block_e, nact, xs, wg, wu, wd)


def _combine_kernel(dest_ref, x_ref, meta_ref, ys_ref, o_ref, buf, sem, *, tm):
    i = pl.program_id(0)

    def issue(r, c):
        for k in range(MOE_TOP_K):
            _row_copy(ys_ref, dest_ref[(i * tm + r) * MOE_TOP_K + k], buf.at[k], r, sem).start()
        return c

    lax.fori_loop(0, tm, issue, 0)

    def drain(r, c):
        for k in range(MOE_TOP_K):
            _row_copy(ys_ref, 0, buf.at[k], 0, sem).wait()
        return c

    lax.fori_loop(0, tm, drain, 0)
    meta = meta_ref[...]
    o_ref[...] = x_ref[...] + meta[:, 2:3] * buf[0] + meta[:, 3:4] * buf[1]


def _combine(dest_flat, x, meta, ys, *, tm):
    T, D = x.shape
    return pl.pallas_call(
        functools.partial(_combine_kernel, tm=tm),
        grid_spec=pltpu.PrefetchScalarGridSpec(
            num_scalar_prefetch=1, grid=(T // tm,),
            in_specs=[pl.BlockSpec((tm, D), lambda i, d: (i, 0)),
                      pl.BlockSpec((tm, LANES), lambda i, d: (i, 0)),
                      pl.BlockSpec(memory_space=pl.ANY)],
            out_specs=pl.BlockSpec((tm, D), lambda i, d: (i, 0)),
            scratch_shapes=[pltpu.VMEM((MOE_TOP_K, tm, D), F32), pltpu.SemaphoreType.DMA(())]),
        out_shape=jax.ShapeDtypeStruct((T, D), F32),
        compiler_params=_cparams("arbitrary"),
        name="moe_combine",
    )(dest_flat, x, meta, ys)


def _moe(x, g, w_r, b_r, wg, wu, wd, *, tm):
    T, D = x.shape
    E = w_r.shape[1]
    bm = 512 if T * MOE_TOP_K >= 8 * 512 else 128
    tf = _row_tile(512, wg.shape[2])
    wr = jnp.zeros((D, LANES), F32).at[:, :E].set(w_r)
    wrh = wr.astype(BF16)
    wrl = (wr - wrh.astype(F32)).astype(BF16)
    br = jnp.zeros((1, LANES), F32).at[0, :E].set(b_r)
    meta, cnt = _router(x, g, wrh, wrl, br, tm=min(tm, 512), E=E)
    counts = cnt[0, :E].astype(jnp.int32)
    padded = (counts + bm - 1) // bm * bm
    ends = jnp.cumsum(padded)
    starts = ends - padded
    e_tok = meta[:, 0:MOE_TOP_K].astype(jnp.int32)
    pos = meta[:, 4:4 + MOE_TOP_K].astype(jnp.int32)
    dest = (starts[e_tok] + pos).reshape(-1)
    nblk = -(-(T * MOE_TOP_K) // bm) + E
    nact = (ends[-1] // bm).astype(jnp.int32).reshape(1)
    blk = jnp.arange(nblk, dtype=jnp.int32)
    be = jnp.minimum(jnp.searchsorted(ends, blk * bm, side="right"), E - 1).astype(jnp.int32)
    last_e = jnp.max(jnp.where(padded > 0, jnp.arange(E, dtype=jnp.int32), 0))
    be = jnp.where(blk < nact[0], be, last_e)
    xs = _dispatch(dest, x, g, jnp.zeros((nblk * bm, D), F32), tm=min(tm, 256))
    ys = _experts(be, nact, xs, wg, wu, wd, bm=bm, tf=tf)
    return _combine(dest, x, meta, ys, tm=min(tm, 256))


def _pad_cols(a, n):
    return jnp.pad(a, ((0, 0), (0, n - a.shape[1])))


def _rope_pair(w, rope):
    half = rope // 2
    return _pad_cols(w, LANES), _pad_cols(jnp.concatenate([w[:, half:], w[:, :half]], axis=1), LANES)


def _s5_params(lam_re, lam_im, log_dt, b_re, b_im, c_re, c_im, d, w_glu, b_glu):
    G, P, C = b_re.shape
    gpb = LANES // C
    ngb = G // gpb
    dt = jnp.exp(log_dt)[:, None]
    er = jnp.exp(lam_re * dt)
    lbr = er * jnp.cos(lam_im * dt)
    lbi = er * jnp.sin(lam_im * dt)
    den = lam_re * lam_re + lam_im * lam_im
    cr = ((lbr - 1.0) * lam_re + lbi * lam_im) / den
    ci = (lbi * lam_re - (lbr - 1.0) * lam_im) / den
    bbr = cr[:, :, None] * b_re - ci[:, :, None] * b_im
    bbi = cr[:, :, None] * b_im + ci[:, :, None] * b_re
    eye = jnp.eye(gpb, dtype=F32)

    def bdiag_in(b):
        return jnp.einsum("napc,ab->nacbp", b.reshape(ngb, gpb, P, C), eye).reshape(ngb, gpb * C, gpb * P)

    def bdiag_out(c):
        return jnp.einsum("nacp,ab->napbc", c.reshape(ngb, gpb, C, P), eye).reshape(ngb, gpb * P, gpb * C)

    br, bi = bdiag_in(bbr), bdiag_in(bbi)
    brh, bih = br.astype(BF16), bi.astype(BF16)
    return dict(lamr=lbr.reshape(1, G * P), lami=lbi.reshape(1, G * P),
                brh=brh, brl=(br - brh.astype(F32)).astype(BF16),
                bih=bih, bil=(bi - bih.astype(F32)).astype(BF16),
                cr=bdiag_out(c_re).astype(BF16), ci=bdiag_out(-c_im).astype(BF16),
                d=d.reshape(1, -1), wglu=w_glu.astype(BF16), bglu=b_glu.reshape(1, -1))


def kernel(x_prompt, x_sample, cache_mla_ckv, cache_mla_krope, cache_moba_k, cache_moba_v,
           state_ssm_re, state_ssm_im, page_table, norm_mix, w_in, ssm_lam_re, ssm_lam_im,
           ssm_log_dt, ssm_b_re, ssm_b_im, ssm_c_re, ssm_c_im, ssm_d, ssm_w_glu, ssm_b_glu,
           mla_q_a_norm, mla_kv_a_norm, mla_w_uq, mla_w_uk, mla_w_uv, mla_qn_nope, mla_qn_rope,
           mla_kn_nope, mla_kn_rope, moba_qn, moba_kn, out_norm_ssm, out_norm_mla, out_norm_moba,
           w_out, norm_ffn, ffn_w_gate, ffn_w_up, ffn_w_down, moe_w_router, moe_b_router,
           moe_w_gate, moe_w_up, moe_w_down):
    B, S, D = x_prompt.shape
    n, L, _ = x_sample.shape
    depth = norm_mix.shape[0]
    NP = page_table.shape[1]
    page = cache_mla_ckv.shape[2]
    past = NP * page
    kvl = cache_mla_ckv.shape[3]
    rope = cache_mla_krope.shape[3]
    dh = cache_moba_k.shape[3]
    G, P = ssm_lam_re.shape[1:]
    ws = ssm_d.shape[1]
    ql = mla_q_a_norm.shape[1]
    H, nope = mla_w_uk.shape[2:]
    vd = mla_w_uv.shape[3]
    qk = nope + rope
    wm = out_norm_moba.shape[1]
    nh = wm // dh
    Tp, Ts = B * S, n * L
    T = Tp + Ts
    assert dh == LANES and rope <= LANES and nope % LANES == 0 and L % SUBLANES == 0
    assert S % MOBA_BLOCK == 0 and Tp % Ts == 0
    tm = _row_tile(512, Tp, Ts)

    x = jnp.concatenate([x_prompt.reshape(Tp, D), x_sample.reshape(Ts, D)], axis=0)
    pos = jnp.concatenate([jnp.tile(jnp.arange(S), B), jnp.tile(past + jnp.arange(L), n)]).astype(F32)
    inv = ROPE_THETA ** (-jnp.arange(0, rope, 2, dtype=F32) / rope)
    ang = pos[:, None] * inv[None, :]
    cos, sin = jnp.cos(ang), jnp.sin(ang)
    cs = _pad_cols(jnp.concatenate([cos, cos], axis=1), LANES)
    sn = _pad_cols(jnp.concatenate([-sin, sin], axis=1), LANES)
    pt_flat = page_table.reshape(-1).astype(jnp.int32)
    cache_krt = jnp.swapaxes(cache_mla_krope, 2, 3)
    row1 = lambda a: a.reshape(1, -1)
    rope_gain = lambda g: _pad_cols(row1(jnp.concatenate([g, g])), LANES)
    splits = np.cumsum([ws, ql, kvl, rope, wm, dh]).tolist()

    Bp = -(-B // SUBLANES) * SUBLANES
    zeros_state = jnp.zeros((Bp, G * P), F32)
    leaves = {k: [] for k in ("ckv_p", "kr_p", "km_p", "vm_p", "sr_p", "si_p",
                              "ckv_s", "kr_s", "km_s", "vm_s", "sr_s", "si_s")}
    for l in range(depth):
        wu_, wcq, wckv, wkr, wqm, wkm, wvm = jnp.split(w_in[l], splits, axis=1)
        kra, krb = _rope_pair(wkr, rope)
        w1 = jnp.concatenate([wu_, wcq, wckv, kra, krb, wqm, wkm, wvm], axis=1).astype(BF16)
        u, cq, ckv, krp, kr, qm, km, vm = _inproj(
            x, row1(norm_mix[l]), w1, cs, sn, row1(mla_q_a_norm[l]), row1(mla_kv_a_norm[l]),
            row1(moba_qn[l]), row1(moba_kn[l]), tm=tm, ws=ws, ql=ql, kvl=kvl, rope=rope, wm=wm, dh=dh)

        sp = _s5_params(ssm_lam_re[l], ssm_lam_im[l], ssm_log_dt[l], ssm_b_re[l], ssm_b_im[l],
                        ssm_c_re[l], ssm_c_im[l], ssm_d[l], ssm_w_glu[l], ssm_b_glu[l])
        up = jnp.pad(u[:Tp].reshape(B, S, ws).transpose(1, 0, 2), ((0, 0), (0, Bp - B), (0, 0)))
        zp, srp, sip = _s5(up.reshape(S * Bp, ws), zeros_state, zeros_state, sp, S=Bp, Lc=_row_tile(128, S))
        zp = zp.reshape(S, Bp, ws)[:, :B].transpose(1, 0, 2).reshape(Tp, ws)
        srp, sip = srp[:B], sip[:B]
        us = u[Tp:].reshape(n, L, ws).transpose(1, 0, 2).reshape(Ts, ws)
        zs, srs, sis = _s5(us, state_ssm_re[l].reshape(n, G * P), state_ssm_im[l].reshape(n, G * P),
                           sp, S=n, Lc=L)
        zs = zs.reshape(L, n, ws).transpose(1, 0, 2).reshape(Ts, ws)
        ssm_out = jnp.concatenate([zp, zs], axis=0)

        wq = mla_w_uq[l].reshape(ql, H, qk).transpose(1, 0, 2)
        qa_, qb_ = jax.vmap(lambda w: _rope_pair(w, rope))(wq[:, :, nope:])
        wq2 = jnp.concatenate([wq[:, :, :nope], qa_, qb_], axis=2).astype(BF16)
        q = _qproj(cq, wq2, cs, sn, row1(mla_qn_nope[l]), rope_gain(mla_qn_rope[l]), tm=tm, nope=nope, qk=qk)
        wuk = mla_w_uk[l].transpose(1, 0, 2)
        wuv = mla_w_uv[l].transpose(1, 0, 2)
        gkn, gkr = row1(mla_kn_nope[l]), rope_gain(mla_kn_rope[l])
        kp, vp = _kv_prompt(ckv, krp, jnp.concatenate([wuk, wuv], axis=2).astype(BF16), gkn, gkr,
                            rows=Tp, tm=tm, nope=nope, qk=qk)
        mla_p = _flash_prompt(q, kp, vp, B=B, S=S, tq=_row_tile(1024, S), qk=qk)
        qa, qr = _absorb(q, wuk.astype(BF16), gkn, gkr, Tp=Tp, Ts=Ts, nope=nope, rope=rope)
        wukt = wuk.transpose(0, 2, 1).reshape(H * nope, kvl).astype(BF16)
        mla_s = _mla_decode(l, pt_flat, qa, qr, wukt, wuv.astype(BF16), ckv, krp, cache_mla_ckv,
                            cache_krt, n=n, L=L, NP=NP, Tp=Tp, nope=nope, qk=qk)
        mla_out = jnp.concatenate([mla_p, mla_s], axis=0)

        nb = S // MOBA_BLOCK
        nbp = -(-nb // LANES) * LANES
        kmean = _block_means(km, rows=Tp).reshape(B, nb, dh)
        kmean = jnp.pad(kmean, ((0, 0), (0, nbp - nb), (0, 0)))
        moba_p = _moba_prompt(qm, km, vm, kmean, B=B, S=S, nh=nh, dh=dh)
        moba_s = _moba_decode(l, pt_flat, qm, km, vm, cache_moba_k, cache_moba_v,
                              n=n, L=L, NP=NP, Tp=Tp, nh=nh, dh=dh)
        moba_out = jnp.concatenate([moba_p, moba_s], axis=0)

        x = _outproj(x, ssm_out, mla_out, moba_out, row1(out_norm_ssm[l]), row1(out_norm_mla[l]),
                     row1(out_norm_moba[l]), w_out[l].astype(BF16), tm=tm)

        j = l // 2
        if l % 2 == 0:
            F = ffn_w_gate.shape[2]
            Fp = -(-F // 256) * 256
            wg = _pad_cols(ffn_w_gate[j], Fp).astype(BF16)
            wu = _pad_cols(ffn_w_up[j], Fp).astype(BF16)
            wd = jnp.pad(ffn_w_down[j], ((0, Fp - F), (0, 0))).astype(BF16)
            x = _ffn(x, row1(norm_ffn[l]), wg, wu, wd, tm=tm, tf=_row_tile(512, Fp))
        else:
            x = _moe(x, row1(norm_ffn[l]), moe_w_router[j], moe_b_router[j], moe_w_gate[j].astype(BF16),
                     moe_w_up[j].astype(BF16), moe_w_down[j].astype(BF16), tm=tm)

        leaves["ckv_p"].append(ckv[:Tp].reshape(B, S, kvl))
        leaves["kr_p"].append(kr[:Tp].reshape(B, S, rope))
        leaves["km_p"].append(km[:Tp].reshape(B, S, dh))
        leaves["vm_p"].append(vm[:Tp].reshape(B, S, dh))
        leaves["sr_p"].append(srp.reshape(B, G, P))
        leaves["si_p"].append(sip.reshape(B, G, P))
        leaves["ckv_s"].append(ckv[Tp:].reshape(n, L, kvl))
        leaves["kr_s"].append(kr[Tp:].reshape(n, L, rope))
        leaves["km_s"].append(km[Tp:].reshape(n, L, dh))
        leaves["vm_s"].append(vm[Tp:].reshape(n, L, dh))
        leaves["sr_s"].append(srs.reshape(n, G, P))
        leaves["si_s"].append(sis.reshape(n, G, P))

    st = lambda k: jnp.stack(leaves[k])
    sdt = state_ssm_re.dtype
    return (x[:Tp].reshape(B, S, D), x[Tp:].reshape(n, L, D),
            st("ckv_p"), st("kr_p"), st("km_p"), st("vm_p"), st("sr_p").astype(sdt), st("si_p").astype(sdt),
            st("ckv_s"), st("kr_s"), st("km_s"), st("vm_s"), st("sr_s").astype(sdt), st("si_s").astype(sdt))
```

```python
import functools
import math

import jax
import jax.numpy as jnp
import numpy as np
from jax import lax
from jax.experimental import pallas as pl
from jax.experimental.pallas import tpu as pltpu

F32 = jnp.float32
BF16 = jnp.bfloat16

NORM_EPS = 1e-6
ROPE_THETA = 10000.0
MOBA_BLOCK = 256
MOBA_TOPK = 3
MOE_TOP_K = 2
LANES = 128
SUBLANES = 8
VMEM_LIMIT_BYTES = 56 * 1024 * 1024
MLA_PAGES_PER_STEP = 16
MOBA_KEY_BLOCKS_PER_STEP = 4
MOBA_DECODE_CHUNK_BLOCKS = 8
NEG_INF = float("-inf")


def _cparams(*sem):
    return pltpu.CompilerParams(dimension_semantics=sem, vmem_limit_bytes=VMEM_LIMIT_BYTES)


def _dot(a, b):
    return jnp.dot(a, b, preferred_element_type=F32)


def _dot_nt(a, b):
    return lax.dot_general(a, b, (((1,), (1,)), ((), ())), preferred_element_type=F32)


def _split(a):
    hi = a.astype(BF16)
    lo = (a - hi.astype(F32)).astype(BF16)
    return hi, lo


def _dot_hi(ah, al, bh, bl):
    return _dot(ah, bh) + (_dot(ah, bl) + _dot(al, bh))


def _dot_nt_hi(ah, al, bh, bl):
    return _dot_nt(ah, bh) + (_dot_nt(ah, bl) + _dot_nt(al, bh))


def _rms(x, g):
    return x * lax.rsqrt(jnp.mean(x * x, axis=-1, keepdims=True) + NORM_EPS) * g


def _row_tile(cap, *sizes):
    t = cap
    while any(s % t for s in sizes):
        t //= 2
    return t


def _iota(shape, dim):
    return lax.broadcasted_iota(jnp.int32, shape, dim)


def _inproj_kernel(x_ref, gmix_ref, w_ref, cs_ref, sn_ref, gq_ref, gkv_ref, gmq_ref, gmk_ref,
                   u_ref, cq_ref, ckv_ref, krp_ref, kr_ref, qm_ref, km_ref, vm_ref,
                   *, ws, ql, kvl, rope, wm, dh):
    h = _rms(x_ref[...], gmix_ref[...]).astype(BF16)
    o = 0
    u_ref[...] = _dot(h, w_ref[:, o:o + ws])
    o += ws
    cq_ref[...] = _rms(_dot(h, w_ref[:, o:o + ql]), gq_ref[...]).astype(BF16)
    o += ql
    ckv_ref[...] = _rms(_dot(h, w_ref[:, o:o + kvl]), gkv_ref[...])
    o += kvl
    kab = _dot(h, w_ref[:, o:o + 2 * LANES])
    o += 2 * LANES
    kr = kab[:, :LANES] * cs_ref[...] + kab[:, LANES:] * sn_ref[...]
    krp_ref[...] = kr
    kr_ref[...] = kr[:, :rope]
    qm = _dot(h, w_ref[:, o:o + wm])
    o += wm
    for hh in range(wm // dh):
        qm_ref[:, hh * dh:(hh + 1) * dh] = _rms(qm[:, hh * dh:(hh + 1) * dh], gmq_ref[...])
    kv = _dot(h, w_ref[:, o:o + 2 * dh])
    km_ref[...] = _rms(kv[:, :dh], gmk_ref[...])
    vm_ref[...] = kv[:, dh:]


def _inproj(x, gmix, w, cs, sn, gq, gkv, gmq, gmk, *, tm, ws, ql, kvl, rope, wm, dh):
    T, D = x.shape
    NC = w.shape[1]
    row = lambda n: pl.BlockSpec((tm, n), lambda i: (i, 0))
    full = lambda a: pl.BlockSpec(a.shape, lambda i: (0,) * a.ndim)
    outs = [(ws, F32), (ql, BF16), (kvl, F32), (LANES, F32), (rope, F32), (wm, F32), (dh, F32), (dh, F32)]
    return pl.pallas_call(
        functools.partial(_inproj_kernel, ws=ws, ql=ql, kvl=kvl, rope=rope, wm=wm, dh=dh),
        grid=(T // tm,),
        in_specs=[row(D), full(gmix), full(w), row(LANES), row(LANES), full(gq), full(gkv), full(gmq), full(gmk)],
        out_specs=[row(n) for n, _ in outs],
        out_shape=[jax.ShapeDtypeStruct((T, n), dt) for n, dt in outs],
        compiler_params=_cparams("arbitrary"),
        name="mixer_in_proj",
    )(x, gmix, w, cs, sn, gq, gkv, gmq, gmk)


def _qproj_kernel(cq_ref, w_ref, cs_ref, sn_ref, gn_ref, gr_ref, q_ref, *, nope, qk):
    r = _dot(cq_ref[...], w_ref[0])
    qn = r[:, :nope]
    qr = r[:, nope:nope + LANES] * cs_ref[...] + r[:, nope + LANES:] * sn_ref[...]
    ssq = jnp.sum(qn * qn, axis=-1, keepdims=True) + jnp.sum(qr * qr, axis=-1, keepdims=True)
    rinv = lax.rsqrt(ssq / qk + NORM_EPS)
    q_ref[0, :, :nope] = (qn * rinv * gn_ref[...]).astype(BF16)
    q_ref[0, :, nope:] = (qr * rinv * gr_ref[...]).astype(BF16)


def _qproj(cq, w, cs, sn, gn, gr, *, tm, nope, qk):
    T, ql = cq.shape
    H = w.shape[0]
    return pl.pallas_call(
        functools.partial(_qproj_kernel, nope=nope, qk=qk),
        grid=(T // tm, H),
        in_specs=[pl.BlockSpec((tm, ql), lambda i, h: (i, 0)),
                  pl.BlockSpec((1,) + w.shape[1:], lambda i, h: (h, 0, 0)),
                  pl.BlockSpec((tm, LANES), lambda i, h: (i, 0)),
                  pl.BlockSpec((tm, LANES), lambda i, h: (i, 0)),
                  pl.BlockSpec(gn.shape, lambda i, h: (0, 0)),
                  pl.BlockSpec(gr.shape, lambda i, h: (0, 0))],
        out_specs=pl.BlockSpec((1, tm, nope + LANES), lambda i, h: (h, i, 0)),
        out_shape=jax.ShapeDtypeStruct((H, T, nope + LANES), BF16),
        compiler_params=_cparams("arbitrary", "arbitrary"),
        name="mla_q_proj",
    )(cq, w, cs, sn, gn, gr)


def _kvp_kernel(ckv_ref, krp_ref, w_ref, gn_ref, gr_ref, k_ref, v_ref, *, nope, qk):
    r = _dot(ckv_ref[...].astype(BF16), w_ref[0])
    kn = r[:, :nope]
    kr = krp_ref[...]
    ssq = jnp.sum(kn * kn, axis=-1, keepdims=True) + jnp.sum(kr * kr, axis=-1, keepdims=True)
    rinv = lax.rsqrt(ssq / qk + NORM_EPS)
    k_ref[0, :, :nope] = (kn * rinv * gn_ref[...]).astype(BF16)
    k_ref[0, :, nope:] = (kr * rinv * gr_ref[...]).astype(BF16)
    v_ref[0] = r[:, nope:].astype(BF16)


def _kv_prompt(ckv, krp, w, gn, gr, *, rows, tm, nope, qk):
    kvl = ckv.shape[1]
    H, _, nv = w.shape
    vd = nv - nope
    return pl.pallas_call(
        functools.partial(_kvp_kernel, nope=nope, qk=qk),
        grid=(rows // tm, H),
        in_specs=[pl.BlockSpec((tm, kvl), lambda i, h: (i, 0)),
                  pl.BlockSpec((tm, LANES), lambda i, h: (i, 0)),
                  pl.BlockSpec((1, kvl, nv), lambda i, h: (h, 0, 0)),
                  pl.BlockSpec(gn.shape, lambda i, h: (0, 0)),
                  pl.BlockSpec(gr.shape, lambda i, h: (0, 0))],
        out_specs=[pl.BlockSpec((1, tm, nope + LANES), lambda i, h: (h, i, 0)),
                   pl.BlockSpec((1, tm, vd), lambda i, h: (h, i, 0))],
        out_shape=[jax.ShapeDtypeStruct((H, rows, nope + LANES), BF16),
                   jax.ShapeDtypeStruct((H, rows, vd), BF16)],
        compiler_params=_cparams("arbitrary", "arbitrary"),
        name="mla_kv_prompt",
    )(ckv, krp, w, gn, gr)


def _flash_kernel(q_ref, k_ref, v_ref, o_ref, m_ref, l_ref, acc_ref, *, scale):
    i = pl.program_id(2)
    j = pl.program_id(3)

    @pl.when(j == 0)
    def _():
        m_ref[...] = jnp.full(m_ref.shape, NEG_INF, F32)
        l_ref[...] = jnp.zeros(l_ref.shape, F32)
        acc_ref[...] = jnp.zeros(acc_ref.shape, F32)

    def update(diagonal):
        s = _dot_nt(q_ref[0], k_ref[0]) * scale
        if diagonal:
            s = jnp.where(_iota(s.shape, 1) <= _iota(s.shape, 0), s, NEG_INF)
        m_prev = m_ref[...]
        m_new = jnp.maximum(m_prev, jnp.max(s, axis=-1, keepdims=True))
        alpha = jnp.exp(m_prev - m_new)
        p = jnp.exp(s - m_new)
        l_ref[...] = alpha * l_ref[...] + jnp.sum(p, axis=-1, keepdims=True)
        acc_ref[...] = alpha * acc_ref[...] + _dot(p.astype(BF16), v_ref[0])
        m_ref[...] = m_new

    @pl.when(j < i)
    def _():
        update(False)

    @pl.when(j == i)
    def _():
        update(True)
        o_ref[...] = acc_ref[...] / l_ref[...]


def _flash_prompt(q, k, v, *, B, S, tq, qk):
    H, _, dq = q.shape
    vd = v.shape[2]
    nq = S // tq
    return pl.pallas_call(
        functools.partial(_flash_kernel, scale=qk ** -0.5),
        grid=(B, H, nq, nq),
        in_specs=[pl.BlockSpec((1, tq, dq), lambda b, h, i, j: (h, b * nq + i, 0)),
                  pl.BlockSpec((1, tq, dq), lambda b, h, i, j: (h, b * nq + jnp.minimum(j, i), 0)),
                  pl.BlockSpec((1, tq, vd), lambda b, h, i, j: (h, b * nq + jnp.minimum(j, i), 0))],
        out_specs=pl.BlockSpec((tq, vd), lambda b, h, i, j: (b * nq + i, h)),
        out_shape=jax.ShapeDtypeStruct((B * S, H * vd), F32),
        scratch_shapes=[pltpu.VMEM((tq, 1), F32), pltpu.VMEM((tq, 1), F32), pltpu.VMEM((tq, vd), F32)],
        compiler_params=_cparams("arbitrary", "arbitrary", "arbitrary", "arbitrary"),
        name="mla_prompt_attention",
    )(q, k, v)


def _absorb_kernel(q_ref, wuk_ref, gn_ref, gr_ref, qa_ref, qr_ref, *, nope, rope):
    q = q_ref[0].astype(F32)
    qn = (q[:, :nope] * gn_ref[...]).astype(BF16)
    qa_ref[0] = _dot_nt(qn, wuk_ref[0])
    qr_ref[0] = (q[:, nope:] * gr_ref[...])[:, :rope]


def _absorb(q, wuk, gn, gr, *, Tp, Ts, nope, rope):
    H, _, dq = q.shape
    kvl = wuk.shape[1]
    blk = Tp // Ts
    return pl.pallas_call(
        functools.partial(_absorb_kernel, nope=nope, rope=rope),
        grid=(H,),
        in_specs=[pl.BlockSpec((1, Ts, dq), lambda h: (h, blk, 0)),
                  pl.BlockSpec((1, kvl, nope), lambda h: (h, 0, 0)),
                  pl.BlockSpec(gn.shape, lambda h: (0, 0)),
                  pl.BlockSpec(gr.shape, lambda h: (0, 0))],
        out_specs=[pl.BlockSpec((1, Ts, kvl), lambda h: (h, 0, 0)),
                   pl.BlockSpec((1, Ts, rope), lambda h: (h, 0, 0))],
        out_shape=[jax.ShapeDtypeStruct((H, Ts, kvl), F32), jax.ShapeDtypeStruct((H, Ts, rope), F32)],
        compiler_params=_cparams("arbitrary"),
        name="mla_decode_absorb",
    )(q, wuk, gn, gr)


def _mla_dec_kernel(pt_ref, qa_ref, qr_ref, wukt_ref, wuv_ref, cnew_ref, krnew_ref, c_hbm, kr_hbm, o_ref,
                    wext, qr16, m_ref, l_ref, acc_ref, cpad, krpad, cbuf, krbuf, csem, ksem,
                    *, layer, n, NP, pps, H, L, nope, qk, rope, vd):
    i = pl.program_id(0)
    j = pl.program_id(1)
    nj = NP // pps
    t = i * nj + j
    slot = t % 2
    HL = H * L
    scale = qk ** -0.5
    kvl = cpad.shape[1]

    def page_copies(seq, jj, s):
        for p in range(pps):
            pg = pt_ref[seq * NP + jj * pps + p]
            yield pltpu.make_async_copy(c_hbm.at[layer, pg], cbuf.at[s, p], csem.at[s])
            yield pltpu.make_async_copy(kr_hbm.at[layer, pg], krbuf.at[s, p], ksem.at[s])

    @pl.when(t == 0)
    def _():
        for cp in page_copies(0, 0, 0):
            cp.start()

    @pl.when(t + 1 < n * nj)
    def _():
        wrap = j + 1 == nj
        for cp in page_copies(jnp.where(wrap, i + 1, i), jnp.where(wrap, 0, j + 1), 1 - slot):
            cp.start()

    def scores(c16, krt):
        kt = _dot_nt(wext[...], c16)
        ksq = jnp.sum(krt * krt, axis=0, keepdims=True)
        sr = _dot(qr16[...], krt.astype(BF16))
        parts = []
        for h in range(H):
            kk = kt[h * nope:(h + 1) * nope]
            ssq = jnp.sum(kk * kk, axis=0, keepdims=True)
            rinv = lax.rsqrt((ssq + ksq) / qk + NORM_EPS)
            parts.append((kt[H * nope + h * L:H * nope + (h + 1) * L] + sr[h * L:(h + 1) * L]) * (rinv * scale))
        return jnp.concatenate(parts, axis=0)

    def update(s, c16):
        m_prev = m_ref[...]
        m_new = jnp.maximum(m_prev, jnp.max(s, axis=-1, keepdims=True))
        alpha = jnp.exp(m_prev - m_new)
        p = jnp.exp(s - m_new)
        l_ref[...] = alpha * l_ref[...] + jnp.sum(p, axis=-1, keepdims=True)
        acc_ref[...] = alpha * acc_ref[...] + _dot(p.astype(BF16), c16)
        m_ref[...] = m_new

    @pl.when(j == 0)
    def _():
        wext[:H * nope, :] = wukt_ref[...]
        wext[H * nope:, :] = qa_ref[...].reshape(HL, kvl).astype(BF16)
        qr16[...] = qr_ref[...].reshape(HL, rope).astype(BF16)
        m_ref[...] = jnp.full(m_ref.shape, NEG_INF, F32)
        l_ref[...] = jnp.zeros(l_ref.shape, F32)
        acc_ref[...] = jnp.zeros(acc_ref.shape, F32)
        cpad[...] = jnp.zeros(cpad.shape, F32)
        krpad[...] = jnp.zeros(krpad.shape, F32)
        cpad[:L, :] = cnew_ref[...]
        krpad[:L, :] = krnew_ref[...]
        c16 = cpad[...].astype(BF16)
        s = scores(c16, krpad[...].T[:rope])
        row = _iota(s.shape, 0)
        col = _iota(s.shape, 1)
        s = jnp.where(col <= (row % L), s, NEG_INF)
        update(s, c16)

    for cp in page_copies(i, j, slot):
        cp.wait()
    c16 = cbuf[slot].reshape(pps * cbuf.shape[2], kvl).astype(BF16)
    krt = jnp.concatenate([krbuf[slot, p] for p in range(pps)], axis=1)
    update(scores(c16, krt), c16)

    @pl.when(j == pl.num_programs(1) - 1)
    def _():
        o = (acc_ref[...] / l_ref[...]).astype(BF16)
        for h in range(H):
            o_ref[:, h * vd:(h + 1) * vd] = _dot(o[h * L:(h + 1) * L], wuv_ref[h])


def _mla_decode(layer, pt_flat, qa, qr, wukt, wuv, ckv_all, krp_all, cache_ckv, cache_krt,
                *, n, L, NP, Tp, nope, qk):
    H, Ts, kvl = qa.shape
    rope = qr.shape[2]
    vd = wuv.shape[2]
    page = cache_ckv.shape[2]
    pps = _row_tile(MLA_PAGES_PER_STEP, NP)
    nj = NP // pps
    HL = H * L
    off = Tp // L
    in_specs = [
        pl.BlockSpec((H, L, kvl), lambda i, j, pt: (0, i, 0)),
        pl.BlockSpec((H, L, rope), lambda i, j, pt: (0, i, 0)),
        pl.BlockSpec(wukt.shape, lambda i, j, pt: (0, 0)),
        pl.BlockSpec(wuv.shape, lambda i, j, pt: (0, 0, 0)),
        pl.BlockSpec((L, kvl), lambda i, j, pt: (off + i, 0)),
        pl.BlockSpec((L, LANES), lambda i, j, pt: (off + i, 0)),
        pl.BlockSpec(memory_space=pl.ANY),
        pl.BlockSpec(memory_space=pl.ANY),
    ]
    return pl.pallas_call(
        functools.partial(_mla_dec_kernel, layer=layer, n=n, NP=NP, pps=pps, H=H, L=L, nope=nope, qk=qk,
                          rope=rope, vd=vd),
        grid_spec=pltpu.PrefetchScalarGridSpec(
            num_scalar_prefetch=1, grid=(n, nj), in_specs=in_specs,
            out_specs=pl.BlockSpec((L, H * vd), lambda i, j, pt: (i, 0)),
            scratch_shapes=[pltpu.VMEM((H * nope + HL, kvl), BF16), pltpu.VMEM((HL, rope), BF16),
                            pltpu.VMEM((HL, 1), F32), pltpu.VMEM((HL, 1), F32), pltpu.VMEM((HL, kvl), F32),
                            pltpu.VMEM((LANES, kvl), F32), pltpu.VMEM((LANES, LANES), F32),
                            pltpu.VMEM((2, pps, page, kvl), F32), pltpu.VMEM((2, pps, rope, page), F32),
                            pltpu.SemaphoreType.DMA((2,)), pltpu.SemaphoreType.DMA((2,))]),
        out_shape=jax.ShapeDtypeStruct((Ts, H * vd), F32),
        compiler_params=_cparams("arbitrary", "arbitrary"),
        name="mla_decode_attention",
    )(pt_flat, qa, qr, wukt, wuv, ckv_all, krp_all, cache_ckv, cache_krt)


def _alibi_slopes(nh):
    return [float(2.0 ** (-8.0 * (h + 1) / nh)) for h in range(nh)]


def _select_blocks(g, n_valid, nb):
    lane = _iota(g.shape, 1)
    gm = jnp.where(lane < n_valid, g, NEG_INF)
    ones = jnp.ones((g.shape[1], g.shape[1]), BF16)
    sel = jnp.zeros(g.shape, F32)
    for jj in range(nb):
        col = gm[:, jj:jj + 1]
        beats = (gm > col) | ((gm == col) & (lane < jj))
        cnt = _dot(beats.astype(BF16), ones)
        sel = jnp.where((lane == jj) & (cnt < MOBA_TOPK) & (lane < n_valid), 1.0, sel)
    return sel


def _kmean_kernel(k_ref, o_ref):
    o_ref[0] = jnp.mean(k_ref[...], axis=0, keepdims=True)


def _block_means(km, *, rows):
    dh = km.shape[1]
    nblk = rows // MOBA_BLOCK
    out = pl.pallas_call(
        _kmean_kernel,
        grid=(nblk,),
        in_specs=[pl.BlockSpec((MOBA_BLOCK, dh), lambda i: (i, 0))],
        out_specs=pl.BlockSpec((1, 1, dh), lambda i: (i, 0, 0)),
        out_shape=jax.ShapeDtypeStruct((nblk, 1, dh), F32),
        compiler_params=_cparams("arbitrary"),
        name="moba_block_means",
    )(km)
    return out.reshape(nblk, dh)


def _moba_prompt_kernel(q_ref, k_ref, v_ref, kg_ref, vg_ref, kmean_ref, o_ref, sel_ref, m_ref, l_ref, acc_ref,
                        *, nh, dh, nb, gk):
    i = pl.program_id(1)
    j = pl.program_id(2)
    mb = MOBA_BLOCK
    scale = dh ** -0.5
    slopes = _alibi_slopes(nh)

    def update(h, s, v16):
        m_prev = m_ref[h]
        m_new = jnp.maximum(m_prev, jnp.max(s, axis=-1, keepdims=True))
        alpha = jnp.exp(m_prev - m_new)
        p = jnp.exp(s - m_new)
        l_ref[h] = alpha * l_ref[h] + jnp.sum(p, axis=-1, keepdims=True)
        acc_ref[h] = alpha * acc_ref[h] + _dot(p.astype(BF16), v16)
        m_ref[h] = m_new

    @pl.when(j == 0)
    def _():
        kmh, kml = _split(kmean_ref[0])
        k16 = k_ref[...].astype(BF16)
        v16 = v_ref[...].astype(BF16)
        row = _iota((mb, mb), 0)
        col = _iota((mb, mb), 1)
        dist = (row - col).astype(F32)
        for h in range(nh):
            q = q_ref[:, h * dh:(h + 1) * dh]
            qh, ql = _split(q)
            sel_ref[h] = _select_blocks(_dot_nt_hi(qh, ql, kmh, kml), i, nb)
            m_ref[h] = jnp.full((mb, 1), NEG_INF, F32)
            l_ref[h] = jnp.zeros((mb, 1), F32)
            acc_ref[h] = jnp.zeros((mb, dh), F32)
            s = _dot_nt(qh, k16) * scale - slopes[h] * dist
            update(h, jnp.where(col <= row, s, NEG_INF), v16)

    @pl.when((j > 0) & ((j - 1) * gk < i))
    def _():
        kb0 = (j - 1) * gk
        k16 = kg_ref[...].astype(BF16)
        v16 = vg_ref[...].astype(BF16)
        row = _iota((mb, gk * mb), 0)
        col = _iota((mb, gk * mb), 1)
        dist = (row - col).astype(F32) + ((i - kb0) * mb).astype(F32)
        lane = _iota((mb, sel_ref.shape[2]), 1)
        for h in range(nh):
            sel = sel_ref[h]
            picked = [jnp.broadcast_to(jnp.sum(jnp.where(lane == kb0 + t, sel, 0.0), axis=-1, keepdims=True),
                                       (mb, mb)) for t in range(gk)]
            picked = jnp.concatenate(picked, axis=1) if gk > 1 else picked[0]
            s = _dot_nt(q_ref[:, h * dh:(h + 1) * dh].astype(BF16), k16) * scale - slopes[h] * dist
            update(h, jnp.where(picked > 0.5, s, NEG_INF), v16)

    @pl.when(j == pl.num_programs(2) - 1)
    def _():
        for h in range(nh):
            o_ref[:, h * dh:(h + 1) * dh] = acc_ref[h] / l_ref[h]


def _moba_prompt(qm, km, vm, kmean_pad, *, B, S, nh, dh):
    nb = S // MOBA_BLOCK
    nbp = kmean_pad.shape[1]
    mb = MOBA_BLOCK

    gk = _row_tile(MOBA_KEY_BLOCKS_PER_STEP, nb)
    ng = nb // gk

    def own(b, i, j):
        return (b * nb + i, 0)

    def group(b, i, j):
        last = jnp.maximum(i - 1, 0) // gk
        return (b * ng + jnp.minimum(jnp.maximum(j - 1, 0), last), 0)

    return pl.pallas_call(
        functools.partial(_moba_prompt_kernel, nh=nh, dh=dh, nb=nb, gk=gk),
        grid=(B, nb, 1 + ng),
        in_specs=[pl.BlockSpec((mb, nh * dh), own),
                  pl.BlockSpec((mb, dh), own),
                  pl.BlockSpec((mb, dh), own),
                  pl.BlockSpec((gk * mb, dh), group),
                  pl.BlockSpec((gk * mb, dh), group),
                  pl.BlockSpec((1, nbp, dh), lambda b, i, j: (b, 0, 0))],
        out_specs=pl.BlockSpec((mb, nh * dh), lambda b, i, j: (b * nb + i, 0)),
        out_shape=jax.ShapeDtypeStruct((B * S, nh * dh), F32),
        scratch_shapes=[pltpu.VMEM((nh, mb, nbp), F32), pltpu.VMEM((nh, mb, 1), F32),
                        pltpu.VMEM((nh, mb, 1), F32), pltpu.VMEM((nh, mb, dh), F32)],
        compiler_params=_cparams("arbitrary", "arbitrary", "arbitrary"),
        name="moba_prompt_attention",
    )(qm, km, vm, km, vm, kmean_pad)


def _moba_dec_kernel(pt_ref, q_ref, knew_ref, vnew_ref, kc_hbm, vc_hbm, o_ref,
                     kbuf, vbuf, kmean, s_scr, ksem, vsem, *, layer, n, NP, nh, dh, L, page, past, cb):
    i = pl.program_id(0)
    slot = i % 2
    mb = MOBA_BLOCK
    nb = past // mb
    nbp = kmean.shape[0]
    nchunk = nb // cb
    ct = cb * mb
    R = nh * L
    scale = dh ** -0.5
    slopes = _alibi_slopes(nh)

    def page_copy(hbm, buf, sem, seq, p, s):
        pg = pt_ref[seq * NP + p]
        return pltpu.make_async_copy(hbm.at[layer, pg], buf.at[s, pl.ds(p * page, page)], sem.at[s])

    def fetch(seq, s):
        def body(p, c):
            page_copy(kc_hbm, kbuf, ksem, seq, p, s).start()
            page_copy(vc_hbm, vbuf, vsem, seq, p, s).start()
            return c
        lax.fori_loop(0, NP, body, 0, unroll=8)

    def drain(hbm, buf, sem):
        def body(p, c):
            page_copy(hbm, buf, sem, i, p, slot).wait()
            return c
        lax.fori_loop(0, NP, body, 0, unroll=8)

    @pl.when(i == 0)
    def _():
        kmean[...] = jnp.zeros(kmean.shape, F32)
        fetch(0, 0)

    @pl.when(i + 1 < n)
    def _():
        fetch(i + 1, 1 - slot)

    rowi = _iota((R, 1), 0)
    slope = jnp.zeros((R, 1), F32)
    for h in range(nh):
        slope = jnp.where(rowi // L == h, slopes[h], slope)
    tpos = (past + rowi % L).astype(F32)
    q32 = jnp.concatenate([q_ref[:, h * dh:(h + 1) * dh] for h in range(nh)], axis=0)
    q16 = q32.astype(BF16)

    zpad = jnp.zeros((LANES - L, dh), F32)
    kpad16 = jnp.concatenate([knew_ref[...], zpad], axis=0).astype(BF16)
    vpad16 = jnp.concatenate([vnew_ref[...], zpad], axis=0).astype(BF16)
    s_new = _dot_nt(q16, kpad16) * scale
    coln = _iota(s_new.shape, 1)
    s_new = s_new - slope * (tpos - (past + coln).astype(F32))
    s_new = jnp.where(coln <= (_iota(s_new.shape, 0) % L), s_new, NEG_INF)

    drain(kc_hbm, kbuf, ksem)
    kk = kbuf.at[slot]

    def mean_body(b, c):
        blk = kk[pl.ds(pl.multiple_of(b * mb, mb), mb), :]
        kmean[pl.ds(b, 1), :] = jnp.sum(blk, axis=0, keepdims=True) * (1.0 / mb)
        return c

    lax.fori_loop(0, nb, mean_body, 0, unroll=4)
    kmh, kml = _split(kmean[...])
    qh, ql = _split(q32)
    sel = _select_blocks(_dot_nt_hi(qh, ql, kmh, kml), nb, nb)
    lane = _iota((R, nbp), 1)
    colc = _iota((R, ct), 1)

    def score_body(c, m_run):
        k16 = kk[pl.ds(pl.multiple_of(c * ct, ct), ct), :].astype(BF16)
        pos = (c * ct + colc).astype(F32)
        s = _dot_nt(q16, k16) * scale - slope * (tpos - pos)
        picked = [jnp.broadcast_to(jnp.sum(jnp.where(lane == c * cb + t, sel, 0.0), axis=-1, keepdims=True),
                                   (R, mb)) for t in range(cb)]
        picked = jnp.concatenate(picked, axis=1) if cb > 1 else picked[0]
        s = jnp.where(picked > 0.5, s, NEG_INF)
        s_scr[c] = s
        return jnp.maximum(m_run, jnp.max(s, axis=-1, keepdims=True))

    m = lax.fori_loop(0, nchunk, score_body, jnp.max(s_new, axis=-1, keepdims=True))
    p_new = jnp.exp(s_new - m)

    drain(vc_hbm, vbuf, vsem)
    vv = vbuf.at[slot]

    def pv_body(c, carry):
        l_run, acc = carry
        p = jnp.exp(s_scr[c] - m)
        v16 = vv[pl.ds(pl.multiple_of(c * ct, ct), ct), :].astype(BF16)
        return l_run + jnp.sum(p, axis=-1, keepdims=True), acc + _dot(p.astype(BF16), v16)

    l_run, acc = lax.fori_loop(0, nchunk, pv_body,
                               (jnp.sum(p_new, axis=-1, keepdims=True), _dot(p_new.astype(BF16), vpad16)))
    o = acc / l_run
    for h in range(nh):
        o_ref[:, h * dh:(h + 1) * dh] = o[h * L:(h + 1) * L]


def _moba_decode(layer, pt_flat, qm, km, vm, cache_k, cache_v, *, n, L, NP, Tp, nh, dh):
    page = cache_k.shape[2]
    past = NP * page
    assert past % MOBA_BLOCK == 0 and L <= MOBA_BLOCK and MOBA_BLOCK % page == 0
    nb = past // MOBA_BLOCK
    nbp = -(-nb // LANES) * LANES
    cb = _row_tile(MOBA_DECODE_CHUNK_BLOCKS, nb)
    Ts = n * L
    off = Tp // L
    R = nh * L
    return pl.pallas_call(
        functools.partial(_moba_dec_kernel, layer=layer, n=n, NP=NP, nh=nh, dh=dh, L=L, page=page,
                          past=past, cb=cb),
        grid_spec=pltpu.PrefetchScalarGridSpec(
            num_scalar_prefetch=1, grid=(n,),
            in_specs=[pl.BlockSpec((L, nh * dh), lambda i, pt: (off + i, 0)),
                      pl.BlockSpec((L, dh), lambda i, pt: (off + i, 0)),
                      pl.BlockSpec((L, dh), lambda i, pt: (off + i, 0)),
                      pl.BlockSpec(memory_space=pl.ANY),
                      pl.BlockSpec(memory_space=pl.ANY)],
            out_specs=pl.BlockSpec((L, nh * dh), lambda i, pt: (i, 0)),
            scratch_shapes=[pltpu.VMEM((2, past, dh), F32), pltpu.VMEM((2, past, dh), F32),
                            pltpu.VMEM((nbp, dh), F32), pltpu.VMEM((nb // cb, R, cb * MOBA_BLOCK), F32),
                            pltpu.SemaphoreType.DMA((2,)), pltpu.SemaphoreType.DMA((2,))]),
        out_shape=jax.ShapeDtypeStruct((Ts, nh * dh), F32),
        compiler_params=_cparams("arbitrary"),
        name="moba_decode_attention",
    )(pt_flat, qm, km, vm, cache_k, cache_v)


def _s5_kernel(u_ref, h0r_ref, h0i_ref, lamr_ref, lami_ref, brh_ref, brl_ref, bih_ref, bil_ref,
               cr_ref, ci_ref, d_ref, wglu_ref, bglu_ref, z_ref, hro_ref, hio_ref,
               re, im, hr, hi, *, S, Lc, ngb, kb, nbk):
    c = pl.program_id(0)

    @pl.when(c == 0)
    def _():
        hr[...] = h0r_ref[...]
        hi[...] = h0i_ref[...]

    u = u_ref[...]
    uh, ul = _split(u)
    for g in range(ngb):
        si = slice(g * kb, (g + 1) * kb)
        so = slice(g * nbk, (g + 1) * nbk)
        re[:, so] = _dot_hi(uh[:, si], ul[:, si], brh_ref[g], brl_ref[g])
        im[:, so] = _dot_hi(uh[:, si], ul[:, si], bih_ref[g], bil_ref[g])

    lamr = lamr_ref[...]
    lami = lami_ref[...]

    def step(t, carry):
        start = t * S
        if S % SUBLANES == 0:
            start = pl.multiple_of(start, SUBLANES)
        rows = pl.ds(start, S)
        pr = hr[...]
        pi = hi[...]
        nr = lamr * pr - lami * pi + re[rows, :]
        ni = lamr * pi + lami * pr + im[rows, :]
        hr[...] = nr
        hi[...] = ni
        re[rows, :] = nr
        im[rows, :] = ni
        return carry

    lax.fori_loop(0, Lc, step, 0)

    ys = []
    for g in range(ngb):
        si = slice(g * kb, (g + 1) * kb)
        so = slice(g * nbk, (g + 1) * nbk)
        ys.append(_dot(re[:, so].astype(BF16), cr_ref[g]) + _dot(im[:, so].astype(BF16), ci_ref[g])
                  + d_ref[:, si] * u[:, si])
    y = jnp.concatenate(ys, axis=1) if ngb > 1 else ys[0]
    z = jax.nn.gelu(y, approximate=True)
    z_ref[...] = z * jax.nn.sigmoid(_dot(z.astype(BF16), wglu_ref[...]) + bglu_ref[...])

    @pl.when(c == pl.num_programs(0) - 1)
    def _():
        hro_ref[...] = hr[...]
        hio_ref[...] = hi[...]


def _s5(u_tm, h0r, h0i, p, *, S, Lc):
    rows_total, W = u_tm.shape
    NS = h0r.shape[1]
    ngb, kb, nbk = p["brh"].shape
    rows = S * Lc
    full = lambda a: pl.BlockSpec(a.shape, lambda c: (0,) * a.ndim)
    ws = [p["lamr"], p["lami"], p["brh"], p["brl"], p["bih"], p["bil"], p["cr"], p["ci"],
          p["d"], p["wglu"], p["bglu"]]
    return pl.pallas_call(
        functools.partial(_s5_kernel, S=S, Lc=Lc, ngb=ngb, kb=kb, nbk=nbk),
        grid=(rows_total // rows,),
        in_specs=[pl.BlockSpec((rows, W), lambda c: (c, 0)), full(h0r), full(h0i)] + [full(a) for a in ws],
        out_specs=[pl.BlockSpec((rows, W), lambda c: (c, 0)), full(h0r), full(h0i)],
        out_shape=[jax.ShapeDtypeStruct((rows_total, W), F32),
                   jax.ShapeDtypeStruct(h0r.shape, F32), jax.ShapeDtypeStruct(h0r.shape, F32)],
        scratch_shapes=[pltpu.VMEM((rows, NS), F32), pltpu.VMEM((rows, NS), F32),
                        pltpu.VMEM((S, NS), F32), pltpu.VMEM((S, NS), F32)],
        compiler_params=_cparams("arbitrary"),
        name="s5_scan",
    )(u_tm, h0r, h0i, *ws)


def _outproj_kernel(x_ref, a_ref, b_ref, c_ref, ga_ref, gb_ref, gc_ref, w_ref, o_ref, *, wa, wb):
    na = _rms(a_ref[...], ga_ref[...]).astype(BF16)
    nb = _rms(b_ref[...], gb_ref[...]).astype(BF16)
    nc = _rms(c_ref[...], gc_ref[...]).astype(BF16)
    o_ref[...] = (x_ref[...] + _dot(na, w_ref[:wa, :]) + _dot(nb, w_ref[wa:wa + wb, :])
                  + _dot(nc, w_ref[wa + wb:, :]))


def _outproj(x, a, b, c, ga, gb, gc, w, *, tm):
    T, D = x.shape
    row = lambda arr: pl.BlockSpec((tm, arr.shape[1]), lambda i: (i, 0))
    full = lambda arr: pl.BlockSpec(arr.shape, lambda i: (0,) * arr.ndim)
    return pl.pallas_call(
        functools.partial(_outproj_kernel, wa=a.shape[1], wb=b.shape[1]),
        grid=(T // tm,),
        in_specs=[row(x), row(a), row(b), row(c), full(ga), full(gb), full(gc), full(w)],
        out_specs=row(x),
        out_shape=jax.ShapeDtypeStruct((T, D), F32),
        compiler_params=_cparams("arbitrary"),
        name="mixer_out_proj",
    )(x, a, b, c, ga, gb, gc, w)


def _ffn_kernel(x_ref, g_ref, wg_ref, wu_ref, wd_ref, o_ref, h_scr, acc):
    f = pl.program_id(1)

    @pl.when(f == 0)
    def _():
        h_scr[...] = _rms(x_ref[...], g_ref[...]).astype(BF16)
        acc[...] = jnp.zeros(acc.shape, F32)

    h = h_scr[...]
    a = _dot(h, wg_ref[...])
    b = _dot(h, wu_ref[...])
    acc[...] += _dot((jax.nn.silu(a) * b).astype(BF16), wd_ref[...])

    @pl.when(f == pl.num_programs(1) - 1)
    def _():
        o_ref[...] = x_ref[...] + acc[...]


def _ffn(x, g, wg, wu, wd, *, tm, tf):
    T, D = x.shape
    F = wg.shape[1]
    return pl.pallas_call(
        _ffn_kernel,
        grid=(T // tm, F // tf),
        in_specs=[pl.BlockSpec((tm, D), lambda i, f: (i, 0)),
                  pl.BlockSpec(g.shape, lambda i, f: (0, 0)),
                  pl.BlockSpec((D, tf), lambda i, f: (0, f)),
                  pl.BlockSpec((D, tf), lambda i, f: (0, f)),
                  pl.BlockSpec((tf, D), lambda i, f: (f, 0))],
        out_specs=pl.BlockSpec((tm, D), lambda i, f: (i, 0)),
        out_shape=jax.ShapeDtypeStruct((T, D), F32),
        scratch_shapes=[pltpu.VMEM((tm, D), BF16), pltpu.VMEM((tm, D), F32)],
        compiler_params=_cparams("arbitrary", "arbitrary"),
        name="ffn_swiglu",
    )(x, g, wg, wu, wd)


def _router_kernel(x_ref, g_ref, wrh_ref, wrl_ref, br_ref, meta_ref, cnt_ref, carry, *, E):
    i = pl.program_id(0)

    @pl.when(i == 0)
    def _():
        carry[...] = jnp.zeros(carry.shape, F32)

    h = _rms(x_ref[...], g_ref[...])
    hh, hl = _split(h)
    lg = _dot_hi(hh, hl, wrh_ref[...], wrl_ref[...]) + br_ref[...]
    tm = lg.shape[0]
    lane = _iota(lg.shape, 1)
    lanef = lane.astype(F32)
    lg = jnp.where(lane < E, lg, NEG_INF)
    m1 = jnp.max(lg, axis=-1, keepdims=True)
    e1 = jnp.min(jnp.where(lg == m1, lanef, float(LANES)), axis=-1, keepdims=True)
    lg2 = jnp.where(lanef == e1, NEG_INF, lg)
    m2 = jnp.max(lg2, axis=-1, keepdims=True)
    e2 = jnp.min(jnp.where(lg2 == m2, lanef, float(LANES)), axis=-1, keepdims=True)
    t = jnp.exp(m2 - m1)
    g1 = 1.0 / (1.0 + t)
    g2 = t / (1.0 + t)
    oh1 = lanef == e1
    oh2 = lanef == e2
    a = (oh1 | oh2).astype(F32)
    tri = (_iota((tm, tm), 0) > _iota((tm, tm), 1)).astype(BF16)
    before = _dot(tri, a.astype(BF16)) + carry[...]
    p1 = jnp.sum(jnp.where(oh1, before, 0.0), axis=-1, keepdims=True)
    p2 = jnp.sum(jnp.where(oh2, before, 0.0), axis=-1, keepdims=True)
    carry[...] = carry[...] + jnp.sum(a, axis=0, keepdims=True)
    meta = jnp.zeros(lg.shape, F32)
    for k, val in enumerate((e1, e2, g1, g2, p1, p2)):
        meta = jnp.where(lane == k, val, meta)
    meta_ref[...] = meta
    cnt_ref[...] = jnp.broadcast_to(carry[...], cnt_ref.shape)


def _router(x, g, wrh, wrl, br, *, tm, E):
    T, D = x.shape
    full = lambda a: pl.BlockSpec(a.shape, lambda i: (0,) * a.ndim)
    return pl.pallas_call(
        functools.partial(_router_kernel, E=E),
        grid=(T // tm,),
        in_specs=[pl.BlockSpec((tm, D), lambda i: (i, 0)), full(g), full(wrh), full(wrl), full(br)],
        out_specs=[pl.BlockSpec((tm, LANES), lambda i: (i, 0)),
                   pl.BlockSpec((SUBLANES, LANES), lambda i: (0, 0))],
        out_shape=[jax.ShapeDtypeStruct((T, LANES), F32), jax.ShapeDtypeStruct((SUBLANES, LANES), F32)],
        scratch_shapes=[pltpu.VMEM((1, LANES), F32)],
        compiler_params=_cparams("arbitrary"),
        name="moe_router",
    )(x, g, wrh, wrl, br)


def _row_copy(src, s, dst, d, sem):
    return pltpu.make_async_copy(src.at[pl.ds(s, 1)], dst.at[pl.ds(d, 1)], sem)


def _dispatch_kernel(dest_ref, x_ref, g_ref, xs_in, xs_out, hbuf, sem, *, tm):
    del xs_in
    i = pl.program_id(0)
    hbuf[...] = _rms(x_ref[...], g_ref[...])

    def issue(r, c):
        for k in range(MOE_TOP_K):
            _row_copy(hbuf, r, xs_out, dest_ref[(i * tm + r) * MOE_TOP_K + k], sem).start()
        return c

    lax.fori_loop(0, tm, issue, 0)

    def drain(r, c):
        for k in range(MOE_TOP_K):
            _row_copy(hbuf, 0, xs_out, 0, sem).wait()
        return c

    lax.fori_loop(0, tm, drain, 0)


def _dispatch(dest_flat, x, g, xs0, *, tm):
    T, D = x.shape
    return pl.pallas_call(
        functools.partial(_dispatch_kernel, tm=tm),
        grid_spec=pltpu.PrefetchScalarGridSpec(
            num_scalar_prefetch=1, grid=(T // tm,),
            in_specs=[pl.BlockSpec((tm, D), lambda i, d: (i, 0)),
                      pl.BlockSpec(g.shape, lambda i, d: (0, 0)),
                      pl.BlockSpec(memory_space=pl.ANY)],
            out_specs=pl.BlockSpec(memory_space=pl.ANY),
            scratch_shapes=[pltpu.VMEM((tm, D), F32), pltpu.SemaphoreType.DMA(())]),
        out_shape=jax.ShapeDtypeStruct(xs0.shape, xs0.dtype),
        input_output_aliases={3: 0},
        compiler_params=_cparams("arbitrary"),
        name="moe_dispatch",
    )(dest_flat, x, g, xs0)


def _expert_kernel(be_ref, na_ref, xs_ref, wg_ref, wu_ref, wd_ref, y_ref, acc):
    i = pl.program_id(0)
    f = pl.program_id(1)
    last = pl.num_programs(1) - 1
    active = i < na_ref[0]

    @pl.when(f == 0)
    def _():
        acc[...] = jnp.zeros(acc.shape, F32)

    @pl.when(active)
    def _():
        x = xs_ref[...].astype(BF16)
        a = _dot(x, wg_ref[0])
        b = _dot(x, wu_ref[0])
        acc[...] += _dot((jax.nn.silu(a) * b).astype(BF16), wd_ref[0])

    @pl.when(f == last)
    def _():
        y_ref[...] = acc[...]


def _experts(block_e, nact, xs, wg, wu, wd, *, bm, tf):
    NR, D = xs.shape
    E, _, F = wg.shape
    nf = F // tf

    def fidx(i, f, na):
        return jnp.where(i < na[0], f, nf - 1)

    return pl.pallas_call(
        _expert_kernel,
        grid_spec=pltpu.PrefetchScalarGridSpec(
            num_scalar_prefetch=2, grid=(NR // bm, nf),
            in_specs=[pl.BlockSpec((bm, D), lambda i, f, be, na: (jnp.minimum(i, jnp.maximum(na[0] - 1, 0)), 0)),
                      pl.BlockSpec((1, D, tf), lambda i, f, be, na: (be[i], 0, fidx(i, f, na))),
                      pl.BlockSpec((1, D, tf), lambda i, f, be, na: (be[i], 0, fidx(i, f, na))),
                      pl.BlockSpec((1, tf, D), lambda i, f, be, na: (be[i], fidx(i, f, na), 0))],
            out_specs=pl.BlockSpec((bm, D), lambda i, f, be, na: (i, 0)),
            scratch_shapes=[pltpu.VMEM((bm, D), F32)]),
        out_shape=jax.ShapeDtypeStruct((NR, D), F32),
        compiler_params=_cparams("arbitrary", "arbitrary"),
        name="moe_experts",
    )(block_e, nact, xs, wg, wu, wd)


def _combine_kernel(dest_ref, x_ref, meta_ref, ys_ref, o_ref, buf, sem, *, tm):
    i = pl.program_id(0)

    def issue(r, c):
        for k in range(MOE_TOP_K):
            _row_copy(ys_ref, dest_ref[(i * tm + r) * MOE_TOP_K + k], buf.at[k], r, sem).start()
        return c

    lax.fori_loop(0, tm, issue, 0)

    def drain(r, c):
        for k in range(MOE_TOP_K):
            _row_copy(ys_ref, 0, buf.at[k], 0, sem).wait()
        return c

    lax.fori_loop(0, tm, drain, 0)
    meta = meta_ref[...]
    o_ref[...] = x_ref[...] + meta[:, 2:3] * buf[0] + meta[:, 3:4] * buf[1]


def _combine(dest_flat, x, meta, ys, *, tm):
    T, D = x.shape
    return pl.pallas_call(
        functools.partial(_combine_kernel, tm=tm),
        grid_spec=pltpu.PrefetchScalarGridSpec(
            num_scalar_prefetch=1, grid=(T // tm,),
            in_specs=[pl.BlockSpec((tm, D), lambda i, d: (i, 0)),
                      pl.BlockSpec((tm, LANES), lambda i, d: (i, 0)),
                      pl.BlockSpec(memory_space=pl.ANY)],
            out_specs=pl.BlockSpec((tm, D), lambda i, d: (i, 0)),
            scratch_shapes=[pltpu.VMEM((MOE_TOP_K, tm, D), F32), pltpu.SemaphoreType.DMA(())]),
        out_shape=jax.ShapeDtypeStruct((T, D), F32),
        compiler_params=_cparams("arbitrary"),
        name="moe_combine",
    )(dest_flat, x, meta, ys)


def _moe(x, g, w_r, b_r, wg, wu, wd, *, tm):
    T, D = x.shape
    E = w_r.shape[1]
    bm = 512 if T * MOE_TOP_K >= 8 * 512 else 128
    tf = _row_tile(512, wg.shape[2])
    wr = jnp.zeros((D, LANES), F32).at[:, :E].set(w_r)
    wrh = wr.astype(BF16)
    wrl = (wr - wrh.astype(F32)).astype(BF16)
    br = jnp.zeros((1, LANES), F32).at[0, :E].set(b_r)
    meta, cnt = _router(x, g, wrh, wrl, br, tm=min(tm, 512), E=E)
    counts = cnt[0, :E].astype(jnp.int32)
    padded = (counts + bm - 1) // bm * bm
    ends = jnp.cumsum(padded)
    starts = ends - padded
    e_tok = meta[:, 0:MOE_TOP_K].astype(jnp.int32)
    pos = meta[:, 4:4 + MOE_TOP_K].astype(jnp.int32)
    dest = (starts[e_tok] + pos).reshape(-1)
    nblk = -(-(T * MOE_TOP_K) // bm) + E
    nact = (ends[-1] // bm).astype(jnp.int32).reshape(1)
    blk = jnp.arange(nblk, dtype=jnp.int32)
    be = jnp.minimum(jnp.searchsorted(ends, blk * bm, side="right"), E - 1).astype(jnp.int32)
    last_e = jnp.max(jnp.where(padded > 0, jnp.arange(E, dtype=jnp.int32), 0))
    be = jnp.where(blk < nact[0], be, last_e)
    xs = _dispatch(dest, x, g, jnp.zeros((nblk * bm, D), F32), tm=min(tm, 256))
    ys = _experts(be, nact, xs, wg, wu, wd, bm=bm, tf=tf)
    return _combine(dest, x, meta, ys, tm=min(tm, 256))


def _pad_cols(a, n):
    return jnp.pad(a, ((0, 0), (0, n - a.shape[1])))


def _rope_pair(w, rope):
    half = rope // 2
    return _pad_cols(w, LANES), _pad_cols(jnp.concatenate([w[:, half:], w[:, :half]], axis=1), LANES)


def _s5_params(lam_re, lam_im, log_dt, b_re, b_im, c_re, c_im, d, w_glu, b_glu):
    G, P, C = b_re.shape
    gpb = LANES // C
    ngb = G // gpb
    dt = jnp.exp(log_dt)[:, None]
    er = jnp.exp(lam_re * dt)
    lbr = er * jnp.cos(lam_im * dt)
    lbi = er * jnp.sin(lam_im * dt)
    den = lam_re * lam_re + lam_im * lam_im
    cr = ((lbr - 1.0) * lam_re + lbi * lam_im) / den
    ci = (lbi * lam_re - (lbr - 1.0) * lam_im) / den
    bbr = cr[:, :, None] * b_re - ci[:, :, None] * b_im
    bbi = cr[:, :, None] * b_im + ci[:, :, None] * b_re
    eye = jnp.eye(gpb, dtype=F32)

    def bdiag_in(b):
        return jnp.einsum("napc,ab->nacbp", b.reshape(ngb, gpb, P, C), eye).reshape(ngb, gpb * C, gpb * P)

    def bdiag_out(c):
        return jnp.einsum("nacp,ab->napbc", c.reshape(ngb, gpb, C, P), eye).reshape(ngb, gpb * P, gpb * C)

    br, bi = bdiag_in(bbr), bdiag_in(bbi)
    brh, bih = br.astype(BF16), bi.astype(BF16)
    return dict(lamr=lbr.reshape(1, G * P), lami=lbi.reshape(1, G * P),
                brh=brh, brl=(br - brh.astype(F32)).astype(BF16),
                bih=bih, bil=(bi - bih.astype(F32)).astype(BF16),
                cr=bdiag_out(c_re).astype(BF16), ci=bdiag_out(-c_im).astype(BF16),
                d=d.reshape(1, -1), wglu=w_glu.astype(BF16), bglu=b_glu.reshape(1, -1))


def kernel(x_prompt, x_sample, cache_mla_ckv, cache_mla_krope, cache_moba_k, cache_moba_v,
           state_ssm_re, state_ssm_im, page_table, norm_mix, w_in, ssm_lam_re, ssm_lam_im,
           ssm_log_dt, ssm_b_re, ssm_b_im, ssm_c_re, ssm_c_im, ssm_d, ssm_w_glu, ssm_b_glu,
           mla_q_a_norm, mla_kv_a_norm, mla_w_uq, mla_w_uk, mla_w_uv, mla_qn_nope, mla_qn_rope,
           mla_kn_nope, mla_kn_rope, moba_qn, moba_kn, out_norm_ssm, out_norm_mla, out_norm_moba,
           w_out, norm_ffn, ffn_w_gate, ffn_w_up, ffn_w_down, moe_w_router, moe_b_router,
           moe_w_gate, moe_w_up, moe_w_down):
    B, S, D = x_prompt.shape
    n, L, _ = x_sample.shape
    depth = norm_mix.shape[0]
    NP = page_table.shape[1]
    page = cache_mla_ckv.shape[2]
    past = NP * page
    kvl = cache_mla_ckv.shape[3]
    rope = cache_mla_krope.shape[3]
    dh = cache_moba_k.shape[3]
    G, P = ssm_lam_re.shape[1:]
    ws = ssm_d.shape[1]
    ql = mla_q_a_norm.shape[1]
    H, nope = mla_w_uk.shape[2:]
    vd = mla_w_uv.shape[3]
    qk = nope + rope
    wm = out_norm_moba.shape[1]
    nh = wm // dh
    Tp, Ts = B * S, n * L
    T = Tp + Ts
    assert dh == LANES and rope <= LANES and nope % LANES == 0 and L % SUBLANES == 0
    assert S % MOBA_BLOCK == 0 and Tp % Ts == 0
    tm = _row_tile(512, Tp, Ts)

    x = jnp.concatenate([x_prompt.reshape(Tp, D), x_sample.reshape(Ts, D)], axis=0)
    pos = jnp.concatenate([jnp.tile(jnp.arange(S), B), jnp.tile(past + jnp.arange(L), n)]).astype(F32)
    inv = ROPE_THETA ** (-jnp.arange(0, rope, 2, dtype=F32) / rope)
    ang = pos[:, None] * inv[None, :]
    cos, sin = jnp.cos(ang), jnp.sin(ang)
    cs = _pad_cols(jnp.concatenate([cos, cos], axis=1), LANES)
    sn = _pad_cols(jnp.concatenate([-sin, sin], axis=1), LANES)
    pt_flat = page_table.reshape(-1).astype(jnp.int32)
    cache_krt = jnp.swapaxes(cache_mla_krope, 2, 3)
    row1 = lambda a: a.reshape(1, -1)
    rope_gain = lambda g: _pad_cols(row1(jnp.concatenate([g, g])), LANES)
    splits = np.cumsum([ws, ql, kvl, rope, wm, dh]).tolist()

    Bp = -(-B // SUBLANES) * SUBLANES
    zeros_state = jnp.zeros((Bp, G * P), F32)
    leaves = {k: [] for k in ("ckv_p", "kr_p", "km_p", "vm_p", "sr_p", "si_p",
                              "ckv_s", "kr_s", "km_s", "vm_s", "sr_s", "si_s")}
    for l in range(depth):
        wu_, wcq, wckv, wkr, wqm, wkm, wvm = jnp.split(w_in[l], splits, axis=1)
        kra, krb = _rope_pair(wkr, rope)
        w1 = jnp.concatenate([wu_, wcq, wckv, kra, krb, wqm, wkm, wvm], axis=1).astype(BF16)
        u, cq, ckv, krp, kr, qm, km, vm = _inproj(
            x, row1(norm_mix[l]), w1, cs, sn, row1(mla_q_a_norm[l]), row1(mla_kv_a_norm[l]),
            row1(moba_qn[l]), row1(moba_kn[l]), tm=tm, ws=ws, ql=ql, kvl=kvl, rope=rope, wm=wm, dh=dh)

        sp = _s5_params(ssm_lam_re[l], ssm_lam_im[l], ssm_log_dt[l], ssm_b_re[l], ssm_b_im[l],
                        ssm_c_re[l], ssm_c_im[l], ssm_d[l], ssm_w_glu[l], ssm_b_glu[l])
        up = jnp.pad(u[:Tp].reshape(B, S, ws).transpose(1, 0, 2), ((0, 0), (0, Bp - B), (0, 0)))
        zp, srp, sip = _s5(up.reshape(S * Bp, ws), zeros_state, zeros_state, sp, S=Bp, Lc=_row_tile(128, S))
        zp = zp.reshape(S, Bp, ws)[:, :B].transpose(1, 0, 2).reshape(Tp, ws)
        srp, sip = srp[:B], sip[:B]
        us = u[Tp:].reshape(n, L, ws).transpose(1, 0, 2).reshape(Ts, ws)
        zs, srs, sis = _s5(us, state_ssm_re[l].reshape(n, G * P), state_ssm_im[l].reshape(n, G * P),
                           sp, S=n, Lc=L)
        zs = zs.reshape(L, n, ws).transpose(1, 0, 2).reshape(Ts, ws)
        ssm_out = jnp.concatenate([zp, zs], axis=0)

        wq = mla_w_uq[l].reshape(ql, H, qk).transpose(1, 0, 2)
        qa_, qb_ = jax.vmap(lambda w: _rope_pair(w, rope))(wq[:, :, nope:])
        wq2 = jnp.concatenate([wq[:, :, :nope], qa_, qb_], axis=2).astype(BF16)
        q = _qproj(cq, wq2, cs, sn, row1(mla_qn_nope[l]), rope_gain(mla_qn_rope[l]), tm=tm, nope=nope, qk=qk)
        wuk = mla_w_uk[l].transpose(1, 0, 2)
        wuv = mla_w_uv[l].transpose(1, 0, 2)
        gkn, gkr = row1(mla_kn_nope[l]), rope_gain(mla_kn_rope[l])
        kp, vp = _kv_prompt(ckv, krp, jnp.concatenate([wuk, wuv], axis=2).astype(BF16), gkn, gkr,
                            rows=Tp, tm=tm, nope=nope, qk=qk)
        mla_p = _flash_prompt(q, kp, vp, B=B, S=S, tq=_row_tile(1024, S), qk=qk)
        qa, qr = _absorb(q, wuk.astype(BF16), gkn, gkr, Tp=Tp, Ts=Ts, nope=nope, rope=rope)
        wukt = wuk.transpose(0, 2, 1).reshape(H * nope, kvl).astype(BF16)
        mla_s = _mla_decode(l, pt_flat, qa, qr, wukt, wuv.astype(BF16), ckv, krp, cache_mla_ckv,
                            cache_krt, n=n, L=L, NP=NP, Tp=Tp, nope=nope, qk=qk)
        mla_out = jnp.concatenate([mla_p, mla_s], axis=0)

        nb = S // MOBA_BLOCK
        nbp = -(-nb // LANES) * LANES
        kmean = _block_means(km, rows=Tp).reshape(B, nb, dh)
        kmean = jnp.pad(kmean, ((0, 0), (0, nbp - nb), (0, 0)))
        moba_p = _moba_prompt(qm, km, vm, kmean, B=B, S=S, nh=nh, dh=dh)
        moba_s = _moba_decode(l, pt_flat, qm, km, vm, cache_moba_k, cache_moba_v,
                              n=n, L=L, NP=NP, Tp=Tp, nh=nh, dh=dh)
        moba_out = jnp.concatenate([moba_p, moba_s], axis=0)

        x = _outproj(x, ssm_out, mla_out, moba_out, row1(out_norm_ssm[l]), row1(out_norm_mla[l]),
                     row1(out_norm_moba[l]), w_out[l].astype(BF16), tm=tm)

        j = l // 2
        if l % 2 == 0:
            F = ffn_w_gate.shape[2]
            Fp = -(-F // 256) * 256
            wg = _pad_cols(ffn_w_gate[j], Fp).astype(BF16)
            wu = _pad_cols(ffn_w_up[j], Fp).astype(BF16)
            wd = jnp.pad(ffn_w_down[j], ((0, Fp - F), (0, 0))).astype(BF16)
            x = _ffn(x, row1(norm_ffn[l]), wg, wu, wd, tm=tm, tf=_row_tile(512, Fp))
        else:
            x = _moe(x, row1(norm_ffn[l]), moe_w_router[j], moe_b_router[j], moe_w_gate[j].astype(BF16),
                     moe_w_up[j].astype(BF16), moe_w_down[j].astype(BF16), tm=tm)

        leaves["ckv_p"].append(ckv[:Tp].reshape(B, S, kvl))
        leaves["kr_p"].append(kr[:Tp].reshape(B, S, rope))
        leaves["km_p"].append(km[:Tp].reshape(B, S, dh))
        leaves["vm_p"].append(vm[:Tp].reshape(B, S, dh))
        leaves["sr_p"].append(srp.reshape(B, G, P))
        leaves["si_p"].append(sip.reshape(B, G, P))
        leaves["ckv_s"].append(ckv[Tp:].reshape(n, L, kvl))
        leaves["kr_s"].append(kr[Tp:].reshape(n, L, rope))
        leaves["km_s"].append(km[Tp:].reshape(n, L, dh))
        leaves["vm_s"].append(vm[Tp:].reshape(n, L, dh))
        leaves["sr_s"].append(srs.reshape(n, G, P))
        leaves["si_s"].append(sis.reshape(n, G, P))

    st = lambda k: jnp.stack(leaves[k])
    sdt = state_ssm_re.dtype
    return (x[:Tp].reshape(B, S, D), x[Tp:].reshape(n, L, D),
            st("ckv_p"), st("kr_p"), st("km_p"), st("vm_p"), st("sr_p").astype(sdt), st("si_p").astype(sdt),
            st("ckv_s"), st("kr_s"), st("km_s"), st("vm_s"), st("sr_s").astype(sdt), st("si_s").astype(sdt))
```
